```python
import math
import jax
import jax.numpy as jnp
from jax import lax
import numpy as np

D_MODEL = 1024
BATCH = 8
SEQ = 4096
DEPTH = 4

N_EVEN = (DEPTH + 1) // 2
N_ODD = DEPTH // 2
HEAD_DIM = 64
CONV_W = 3
A_W = D_MODEL // 2
B_W = D_MODEL - A_W
POOL_WINDOWS = (2, 4, 8, 16)
N_POOL = len(POOL_WINDOWS)
POOL_G = B_W // N_POOL
EV_IN_W = 3 * A_W + B_W
MIX_W = A_W + B_W
H_FOX = D_MODEL // (2 * HEAD_DIM)
H_MOBA = D_MODEL // (2 * HEAD_DIM)
H_ATT = H_FOX + H_MOBA
ATT_W = H_ATT * HEAD_DIM
OD_IN_W = 3 * ATT_W + H_FOX
ATTN_SCALE = HEAD_DIM ** -0.5
FOX_Q_BLOCK = 128
MOBA_BLOCK = 256
MOBA_TOPK = 3
MOBA_Q_CHUNK = 32
T5_BUCKETS = 32
T5_MAX_DIST = 128
FORGET_BIAS_INIT = 2.0
D_FF = 2816
RMS_EPS = 1e-6

kernel_name = "hybrid_conv_pool_fox_moba_trunk"


def rmsnorm(x, g):
    xf = x.astype(jnp.float32)
    y = xf * lax.rsqrt(jnp.mean(xf * xf, axis=-1, keepdims=True) + RMS_EPS)
    return (y * g.astype(jnp.float32)).astype(x.dtype)


def causal_dwconv(u, w):
    s = u.shape[1]
    up = jnp.pad(u, ((0, 0), (CONV_W - 1, 0), (0, 0)))
    return sum(w[i] * up[:, i:i + s] for i in range(CONV_W))


def multiscale_pool(u, pool_w, pool_scale):
    bsz, s, _ = u.shape
    cs = jnp.cumsum(jnp.pad(u.astype(jnp.float32), ((0, 0), (1, 0), (0, 0))), axis=1)
    pos = jnp.arange(s)
    groups = []
    for g, w in enumerate(POOL_WINDOWS):
        csg = cs[..., g * POOL_G:(g + 1) * POOL_G]
        hi = csg[:, 1:]
        lo = jnp.pad(csg[:, :s + 1 - w], ((0, 0), (w - 1, 0), (0, 0)))
        cnt = jnp.minimum(pos + 1, w).astype(jnp.float32)[None, :, None]
        groups.append((hi - lo) / cnt - u[..., g * POOL_G:(g + 1) * POOL_G].astype(jnp.float32))
    p = jnp.stack(groups, axis=2).astype(u.dtype)
    y = jnp.einsum('bsgc,gcd->bsgd', p, pool_w).reshape(bsz, s, B_W)
    return y * pool_scale


def even_mixer(h, w_in, conv_w, pool_w, pool_scale, w_out):
    z = h @ w_in
    gate_b = z[..., :A_W]
    gate_c = z[..., A_W:2 * A_W]
    val = z[..., 2 * A_W:3 * A_W]
    pool_in = z[..., 3 * A_W:]
    y_a = gate_b * causal_dwconv(gate_c * val, conv_w)
    y_b = multiscale_pool(pool_in, pool_w, pool_scale)
    return jnp.concatenate([y_a, y_b], axis=-1) @ w_out


def t5_bucket(dist):
    dist = jnp.maximum(dist, 0)
    exact = T5_BUCKETS // 2
    d_f = jnp.maximum(dist, 1).astype(jnp.float32)
    log_b = exact + (jnp.log(d_f / exact) / math.log(T5_MAX_DIST / exact)
                     * (T5_BUCKETS - exact)).astype(jnp.int32)
    log_b = jnp.minimum(log_b, T5_BUCKETS - 1)
    return jnp.where(dist < exact, dist, log_b)


def forgetting_attention(q, k, v, log_f):
    bsz, nh, s, dh = q.shape
    fcum = jnp.cumsum(log_f, axis=-1)
    nq = s // FOX_Q_BLOCK
    q_blocks = q.reshape(bsz, nh, nq, FOX_Q_BLOCK, dh).transpose(2, 0, 1, 3, 4)
    f_blocks = fcum.reshape(bsz, nh, nq, FOX_Q_BLOCK).transpose(2, 0, 1, 3)
    kpos = jnp.arange(s)

    def one_block(args):
        i, q_i, f_i = args
        qpos = i * FOX_Q_BLOCK + jnp.arange(FOX_Q_BLOCK)
        logits = jnp.einsum('bhqd,bhkd->bhqk', q_i, k).astype(jnp.float32) * ATTN_SCALE
        logits = logits + f_i[..., :, None] - fcum[..., None, :]
        logits = jnp.where(kpos[None, :] <= qpos[:, None], logits, -jnp.inf)
        p = jax.nn.softmax(logits, axis=-1)
        return jnp.einsum('bhqk,bhkd->bhqd', p.astype(v.dtype), v)

    out = lax.map(one_block, (jnp.arange(nq), q_blocks, f_blocks))
    return out.transpose(1, 2, 0, 3, 4).reshape(bsz, nh, s, dh)


def moba_attention(q, k, v, rel_bias):
    bsz, nh, s, dh = q.shape
    n_blk = -(-s // MOBA_BLOCK)
    pad = n_blk * MOBA_BLOCK - s
    k_p = jnp.pad(k, ((0, 0), (0, 0), (0, pad), (0, 0)))
    v_p = jnp.pad(v, ((0, 0), (0, 0), (0, pad), (0, 0)))
    k_blk = k_p.reshape(bsz, nh, n_blk, MOBA_BLOCK, dh)
    v_blk = v_p.reshape(bsz, nh, n_blk, MOBA_BLOCK, dh)
    k_mean = jnp.mean(k_blk.astype(jnp.float32), axis=3)
    topk = min(MOBA_TOPK, n_blk)
    nq = s // MOBA_Q_CHUNK
    q_chunks = q.reshape(bsz, nh, nq, MOBA_Q_CHUNK, dh).transpose(2, 0, 1, 3, 4)
    blk_ids = jnp.arange(n_blk)
    kpos_in_blk = jnp.arange(MOBA_BLOCK)
    head_ids = jnp.arange(nh)[:, None, None, None]
    bias_t = rel_bias.T
    gather_blocks = jax.vmap(jax.vmap(lambda kb, ix: kb[ix]))

    def one_chunk(args):
        c, q_i = args
        start = c * MOBA_Q_CHUNK
        cur = start // MOBA_BLOCK
        qpos = start + jnp.arange(MOBA_Q_CHUNK)
        gate = jnp.einsum('bhqd,bhnd->bhqn', q_i.astype(jnp.float32), k_mean)
        gate = jnp.where(blk_ids < cur, gate, -jnp.inf)
        _, idx = lax.top_k(gate, topk)
        valid = idx < cur
        k_sel = gather_blocks(k_blk, idx)
        v_sel = gather_blocks(v_blk, idx)
        kpos_sel = idx[..., None] * MOBA_BLOCK + kpos_in_blk
        bias_sel = bias_t[head_ids, t5_bucket(qpos[:, None, None] - kpos_sel)]
        l_sel = jnp.einsum('bhqd,bhqtld->bhqtl', q_i, k_sel).astype(jnp.float32) * ATTN_SCALE
        l_sel = jnp.where(valid[..., None], l_sel + bias_sel, -jnp.inf)
        k_own = lax.dynamic_slice_in_dim(k_p, cur * MOBA_BLOCK, MOBA_BLOCK, axis=2)
        v_own = lax.dynamic_slice_in_dim(v_p, cur * MOBA_BLOCK, MOBA_BLOCK, axis=2)
        kpos_own = cur * MOBA_BLOCK + kpos_in_blk
        dist_own = qpos[:, None] - kpos_own[None, :]
        bias_own = rel_bias[t5_bucket(dist_own)].transpose(2, 0, 1)
        l_own = jnp.einsum('bhqd,bhld->bhql', q_i, k_own).astype(jnp.float32) * ATTN_SCALE
        l_own = jnp.where(dist_own >= 0, l_own + bias_own, -jnp.inf)
        logits = jnp.concatenate(
            [l_sel.reshape(bsz, nh, MOBA_Q_CHUNK, topk * MOBA_BLOCK), l_own], axis=-1)
        p = jax.nn.softmax(logits, axis=-1).astype(v.dtype)
        p_sel = p[..., :topk * MOBA_BLOCK].reshape(bsz, nh, MOBA_Q_CHUNK, topk, MOBA_BLOCK)
        p_own = p[..., topk * MOBA_BLOCK:]
        return (jnp.einsum('bhqtl,bhqtld->bhqd', p_sel, v_sel)
                + jnp.einsum('bhql,bhld->bhqd', p_own, v_own))

    out = lax.map(one_chunk, (jnp.arange(nq), q_chunks))
    return out.transpose(1, 2, 0, 3, 4).reshape(bsz, nh, s, dh)


def odd_mixer(h, w_in, b_f, w_out, rel_bias):
    bsz, s, _ = h.shape
    z = h @ w_in
    qkv = z[..., :3 * ATT_W].reshape(bsz, s, 3, H_ATT, HEAD_DIM).transpose(2, 0, 3, 1, 4)
    q, k, v = qkv[0], qkv[1], qkv[2]
    log_f = jax.nn.log_sigmoid((z[..., 3 * ATT_W:] + b_f).astype(jnp.float32)).transpose(0, 2, 1)
    y_c = forgetting_attention(q[:, :H_FOX], k[:, :H_FOX], v[:, :H_FOX], log_f)
    y_d = moba_attention(q[:, H_FOX:], k[:, H_FOX:], v[:, H_FOX:], rel_bias)
    y = jnp.concatenate([y_c, y_d], axis=1).transpose(0, 2, 1, 3).reshape(bsz, s, ATT_W)
    return y @ w_out


def conv_ffn(h, w_in, conv_w, conv_b, w_out):
    z = h @ w_in
    u, g = z[..., :D_FF], z[..., D_FF:]
    a = causal_dwconv(u, conv_w) + conv_b
    return (jax.nn.silu(a) * g) @ w_out


def setup_inputs(seed: int = 0) -> dict:
    key = jax.random.key(seed)
    ks = jax.random.split(key, 17)
    f32 = jnp.float32

    def nrm(k, shape, scale):
        return jax.random.normal(k, shape, f32) * scale

    out_scale = (2 * DEPTH) ** -0.5
    return {
        "x": nrm(ks[0], (BATCH, SEQ, D_MODEL), 1.0),
        "mix_norm_g": 1.0 + nrm(ks[1], (DEPTH, D_MODEL), 0.05),
        "ffn_norm_g": 1.0 + nrm(ks[2], (DEPTH, D_MODEL), 0.05),
        "final_norm_g": 1.0 + nrm(ks[3], (D_MODEL,), 0.05),
        "ev_w_in": nrm(ks[4], (N_EVEN, D_MODEL, EV_IN_W), D_MODEL ** -0.5),
        "ev_conv_w": nrm(ks[5], (N_EVEN, CONV_W, A_W), CONV_W ** -0.5),
        "ev_pool_w": nrm(ks[6], (N_EVEN, N_POOL, POOL_G, POOL_G), POOL_G ** -0.5),
        "ev_pool_scale": 1.0 + nrm(ks[7], (N_EVEN, B_W), 0.1),
        "ev_w_out": nrm(ks[8], (N_EVEN, MIX_W, D_MODEL), MIX_W ** -0.5 * out_scale),
        "od_w_in": nrm(ks[9], (N_ODD, D_MODEL, OD_IN_W), D_MODEL ** -0.5),
        "od_b_f": FORGET_BIAS_INIT + nrm(ks[10], (N_ODD, H_FOX), 0.5),
        "od_w_out": nrm(ks[11], (N_ODD, ATT_W, D_MODEL), ATT_W ** -0.5 * out_scale),
        "rel_bias": nrm(ks[12], (T5_BUCKETS, H_MOBA), 0.5),
        "ffn_w_in": nrm(ks[13], (DEPTH, D_MODEL, 2 * D_FF), D_MODEL ** -0.5),
        "ffn_conv_w": nrm(ks[14], (DEPTH, CONV_W, D_FF), CONV_W ** -0.5),
        "ffn_conv_b": nrm(ks[15], (DEPTH, D_FF), 0.02),
        "ffn_w_out": nrm(ks[16], (DEPTH, D_FF, D_MODEL), D_FF ** -0.5 * out_scale),
    }


def reference(x, mix_norm_g, ffn_norm_g, final_norm_g, ev_w_in, ev_conv_w, ev_pool_w,
              ev_pool_scale, ev_w_out, od_w_in, od_b_f, od_w_out, rel_bias,
              ffn_w_in, ffn_conv_w, ffn_conv_b, ffn_w_out):
    h = x
    for layer in range(DEPTH):
        hn = rmsnorm(h, mix_norm_g[layer])
        if layer % 2 == 0:
            e = layer // 2
            h = h + even_mixer(hn, ev_w_in[e], ev_conv_w[e], ev_pool_w[e],
                               ev_pool_scale[e], ev_w_out[e])
        else:
            o = layer // 2
            h = h + odd_mixer(hn, od_w_in[o], od_b_f[o], od_w_out[o], rel_bias)
        h = h + conv_ffn(rmsnorm(h, ffn_norm_g[layer]), ffn_w_in[layer],
                         ffn_conv_w[layer], ffn_conv_b[layer], ffn_w_out[layer])
    return rmsnorm(h, final_norm_g)
```

```python
import functools
import math

import jax
import jax.numpy as jnp
from jax import lax
from jax.experimental import pallas as pl
from jax.experimental.pallas import tpu as pltpu

F32 = jnp.float32
BF16 = jnp.bfloat16

D_MODEL = 1024
HEAD_DIM = 64
CONV_W = 3
A_W = D_MODEL // 2
B_W = D_MODEL - A_W
POOL_WINDOWS = (2, 4, 8, 16)
POOL_G = B_W // len(POOL_WINDOWS)
H_FOX = D_MODEL // (2 * HEAD_DIM)
H_MOBA = D_MODEL // (2 * HEAD_DIM)
ATT_W = (H_FOX + H_MOBA) * HEAD_DIM
ATTN_SCALE = HEAD_DIM ** -0.5
MOBA_BLOCK = 256
MOBA_TOPK = 3
T5_BUCKETS = 32
T5_MAX_DIST = 128
D_FF = 2816
RMS_EPS = 1e-6

LANES = 128
SUBLANES = 8
VMEM_LIMIT_BYTES = 56 * 1024 * 1024

HEAD_PAIR_W = 2 * HEAD_DIM
MASK_VALUE = -1e30

TOKEN_TILE = 512
FFN_CHUNK = D_FF // 2
ATT_TILE = MOBA_BLOCK
POOL_HALO = 32


def _rmsnorm(x, g):
    ms = jnp.mean(x * x, axis=-1, keepdims=True)
    return x * lax.rsqrt(ms + RMS_EPS) * g


def _const_spec(shape):
    nd = len(shape)
    return pl.BlockSpec(shape, lambda *_: (0,) * nd, pipeline_mode=pl.Buffered(1))


def _params(semantics):
    return pltpu.CompilerParams(dimension_semantics=semantics, vmem_limit_bytes=VMEM_LIMIT_BYTES)


def _even_kernel(h_ref, g_ref, win_ref, cw_ref, pw_ref, ps_ref, wout_ref, o_ref,
                 cv_sc, u_sc, lvl_sc, tailcv_sc, tailu_sc, *, tile):
    s = pl.program_id(1)
    halo = POOL_HALO

    @pl.when(s == 0)
    def _():
        tailcv_sc[...] = jnp.zeros_like(tailcv_sc)
        tailu_sc[...] = jnp.zeros_like(tailu_sc)

    h = h_ref[...]
    hn = _rmsnorm(h, g_ref[...]).astype(BF16)
    z = jnp.dot(hn, win_ref[...], preferred_element_type=F32)
    gate_b = z[:, :A_W]
    cv = z[:, A_W:2 * A_W] * z[:, 2 * A_W:3 * A_W]
    pool_in = z[:, 3 * A_W:]

    cv_sc[0:SUBLANES, :] = tailcv_sc[...]
    cv_sc[SUBLANES:SUBLANES + tile, :] = cv
    tailcv_sc[...] = cv[tile - SUBLANES:, :]
    conv = (cw_ref[0:1, :] * cv_sc[SUBLANES - 2:SUBLANES - 2 + tile, :]
            + cw_ref[1:2, :] * cv_sc[SUBLANES - 1:SUBLANES - 1 + tile, :]
            + cw_ref[2:3, :] * cv)
    y_a = gate_b * conv

    u_sc[0:halo, :] = tailu_sc[...]
    u_sc[halo:halo + tile, :] = pool_in
    tailu_sc[...] = pool_in[tile - halo:, :]
    pos = s * tile + lax.broadcasted_iota(jnp.int32, (tile, 1), 0)
    ys = [y_a.astype(BF16)]
    for g, w in enumerate(POOL_WINDOWS):
        cols = slice(g * POOL_G, (g + 1) * POOL_G)
        n_lvl = g + 1
        src, src_cols = u_sc, cols
        for lvl in range(1, n_lvl):
            sh = 2 ** (lvl - 1)
            lo = SUBLANES * lvl
            dst = lvl_sc.at[lvl % 2]
            dst[lo:halo + tile, :] = (src[lo:halo + tile, src_cols]
                                      + src[lo - sh:halo + tile - sh, src_cols])
            src, src_cols = dst, slice(None)
        sh = 2 ** (n_lvl - 1)
        wsum = src[halo:halo + tile, src_cols] + src[halo - sh:halo + tile - sh, src_cols]
        cnt = jnp.minimum(pos + 1, w).astype(F32)
        p = wsum / cnt - pool_in[:, cols]
        yb = jnp.dot(p.astype(BF16), pw_ref[g], preferred_element_type=F32) * ps_ref[:, cols]
        ys.append(yb.astype(BF16))
    y = jnp.concatenate(ys, axis=-1)
    o_ref[...] = h + jnp.dot(y, wout_ref[...], preferred_element_type=F32)


def _even_layer(h, g, w_in, conv_w, pool_w, pool_scale, w_out):
    bsz, seq, d = h.shape
    tile = min(TOKEN_TILE, seq)
    assert seq % tile == 0 and tile >= POOL_HALO
    act_spec = pl.BlockSpec((None, tile, d), lambda b, s: (b, s, 0))
    return pl.pallas_call(
        functools.partial(_even_kernel, tile=tile),
        grid=(bsz, seq // tile),
        in_specs=[act_spec, _const_spec((1, d)), _const_spec(w_in.shape), _const_spec(conv_w.shape),
                  _const_spec(pool_w.shape), _const_spec((1, B_W)), _const_spec(w_out.shape)],
        out_specs=act_spec,
        out_shape=jax.ShapeDtypeStruct(h.shape, F32),
        scratch_shapes=[pltpu.VMEM((SUBLANES + tile, A_W), F32),
                        pltpu.VMEM((POOL_HALO + tile, B_W), F32),
                        pltpu.VMEM((2, POOL_HALO + tile, POOL_G), F32),
                        pltpu.VMEM((SUBLANES, A_W), F32),
                        pltpu.VMEM((POOL_HALO, B_W), F32)],
        compiler_params=_params(("arbitrary", "arbitrary")),
        name="even_mixer",
    )(h, g.reshape(1, d), w_in.astype(BF16), conv_w, pool_w.astype(BF16),
      pool_scale.reshape(1, B_W), w_out.astype(BF16))


def _ffn_kernel(h_ref, g_ref, win_ref, cw_ref, cb_ref, wout_ref, fg_ref, o_ref, ext_sc, tail_sc,
                *, tile, final_norm):
    s = pl.program_id(1)

    @pl.when(s == 0)
    def _():
        tail_sc[...] = jnp.zeros_like(tail_sc)

    h = h_ref[...]
    hn = _rmsnorm(h, g_ref[...]).astype(BF16)
    acc = h
    for c in range(D_FF // FFN_CHUNK):
        cols = slice(c * FFN_CHUNK, (c + 1) * FFN_CHUNK)
        gcols = slice(D_FF + c * FFN_CHUNK, D_FF + (c + 1) * FFN_CHUNK)
        u = jnp.dot(hn, win_ref[:, cols], preferred_element_type=F32)
        gt = jnp.dot(hn, win_ref[:, gcols], preferred_element_type=F32)
        ext_sc[0:SUBLANES, :] = tail_sc[:, cols]
        ext_sc[SUBLANES:SUBLANES + tile, :] = u
        tail_sc[:, cols] = u[tile - SUBLANES:, :]
        a = (cw_ref[0:1, cols] * ext_sc[SUBLANES - 2:SUBLANES - 2 + tile, :]
             + cw_ref[1:2, cols] * ext_sc[SUBLANES - 1:SUBLANES - 1 + tile, :]
             + cw_ref[2:3, cols] * u
             + cb_ref[:, cols])
        act = a * (1.0 / (1.0 + jnp.exp(-a))) * gt
        acc = acc + jnp.dot(act.astype(BF16), wout_ref[cols, :], preferred_element_type=F32)
    if final_norm:
        acc = _rmsnorm(acc, fg_ref[...])
    o_ref[...] = acc


def _ffn_layer(h, g, w_in, conv_w, conv_b, w_out, final_g, final_norm):
    bsz, seq, d = h.shape
    tile = min(TOKEN_TILE, seq)
    assert seq % tile == 0
    act_spec = pl.BlockSpec((None, tile, d), lambda b, s: (b, s, 0))
    return pl.pallas_call(
        functools.partial(_ffn_kernel, tile=tile, final_norm=final_norm),
        grid=(bsz, seq // tile),
        in_specs=[act_spec, _const_spec((1, d)), _const_spec(w_in.shape), _const_spec(conv_w.shape),
                  _const_spec((1, D_FF)), _const_spec(w_out.shape), _const_spec((1, d))],
        out_specs=act_spec,
        out_shape=jax.ShapeDtypeStruct(h.shape, F32),
        scratch_shapes=[pltpu.VMEM((SUBLANES + tile, FFN_CHUNK), F32),
                        pltpu.VMEM((SUBLANES, D_FF), F32)],
        compiler_params=_params(("arbitrary", "arbitrary")),
        name="conv_ffn",
    )(h, g.reshape(1, d), w_in.astype(BF16), conv_w, conv_b.reshape(1, D_FF), w_out.astype(BF16),
      final_g.reshape(1, d))


def _qkv_kernel(h_ref, g_ref, w_ref, wf_ref, q_ref, k_ref, v_ref, zf_ref):
    hn = _rmsnorm(h_ref[...], g_ref[...]).astype(BF16)
    for i, o_ref in enumerate((q_ref, k_ref, v_ref)):
        o_ref[...] = jnp.dot(hn, w_ref[:, i * ATT_W:(i + 1) * ATT_W],
                             preferred_element_type=F32).astype(BF16)
    zf_ref[...] = jnp.dot(hn, wf_ref[...], preferred_element_type=F32)


def _qkv_proj(h, g, w_in):
    bsz, seq, d = h.shape
    tile = min(TOKEN_TILE, seq)
    assert seq % tile == 0
    w_qkv = w_in[:, :3 * ATT_W].astype(BF16)
    w_f = jnp.pad(w_in[:, 3 * ATT_W:], ((0, 0), (0, LANES - H_FOX))).astype(BF16)
    act_spec = pl.BlockSpec((None, tile, d), lambda b, s: (b, s, 0))
    head_spec = pl.BlockSpec((None, tile, ATT_W), lambda b, s: (b, s, 0))
    zf_spec = pl.BlockSpec((None, tile, LANES), lambda b, s: (b, s, 0))
    qkv_shape = jax.ShapeDtypeStruct((bsz, seq, ATT_W), BF16)
    return pl.pallas_call(
        _qkv_kernel,
        grid=(bsz, seq // tile),
        in_specs=[act_spec, _const_spec((1, d)), _const_spec(w_qkv.shape), _const_spec(w_f.shape)],
        out_specs=[head_spec, head_spec, head_spec, zf_spec],
        out_shape=[qkv_shape, qkv_shape, qkv_shape, jax.ShapeDtypeStruct((bsz, seq, LANES), F32)],
        compiler_params=_params(("arbitrary", "arbitrary")),
        name="qkv_proj",
    )(h, g.reshape(1, d), w_qkv, w_f)


def _fgate_kernel(zf_ref, bf_ref, f_ref, *, seq):
    zt = jnp.transpose(zf_ref[...])[0:H_FOX, :] + bf_ref[...]
    x = -(jnp.maximum(-zt, 0.0) + jnp.log1p(jnp.exp(-jnp.abs(zt))))
    idx = lax.broadcasted_iota(jnp.int32, (H_FOX, seq), 1)
    sh = 1
    while sh < seq:
        x = x + jnp.where(idx >= sh, pltpu.roll(x, sh, 1), 0.0)
        sh *= 2
    for p in range(H_FOX // 2):
        f_ref[p] = x[2 * p:2 * p + 2, :]


def _fgate_cumsum(zf, b_f):
    bsz, seq, _ = zf.shape
    return pl.pallas_call(
        functools.partial(_fgate_kernel, seq=seq),
        grid=(bsz,),
        in_specs=[pl.BlockSpec((None, seq, LANES), lambda b: (b, 0, 0)), _const_spec((H_FOX, 1))],
        out_specs=pl.BlockSpec((None, H_FOX // 2, 2, seq), lambda b: (b, 0, 0, 0)),
        out_shape=jax.ShapeDtypeStruct((bsz, H_FOX // 2, 2, seq), F32),
        compiler_params=_params(("arbitrary",)),
        name="fgate_cumsum",
    )(zf, b_f.reshape(H_FOX, 1))


def _softmax_init(m_sc, l_sc, acc_sc):
    m_sc[...] = jnp.full_like(m_sc, MASK_VALUE)
    l_sc[...] = jnp.zeros_like(l_sc)
    acc_sc[...] = jnp.zeros_like(acc_sc)


def _softmax_step(hd, s, v, m_sc, l_sc, acc_sc):
    m_prev = m_sc[hd]
    m_new = jnp.maximum(m_prev, jnp.max(s, axis=-1, keepdims=True))
    alpha = jnp.exp(m_prev - m_new)
    p = jnp.exp(s - m_new)
    l_sc[hd] = alpha * l_sc[hd] + jnp.sum(p, axis=-1, keepdims=True)
    acc_sc[hd] = alpha * acc_sc[hd] + jnp.dot(p.astype(BF16), v, preferred_element_type=F32)
    m_sc[hd] = m_new


def _softmax_finish(o_ref, l_sc, acc_sc):
    lane = lax.broadcasted_iota(jnp.int32, (1, HEAD_PAIR_W), 1)
    out = jnp.where(lane < HEAD_DIM, acc_sc[0] / l_sc[0], acc_sc[1] / l_sc[1])
    o_ref[...] = out.astype(o_ref.dtype)


def _split_heads(q):
    lane = lax.broadcasted_iota(jnp.int32, (1, HEAD_PAIR_W), 1)
    qs = q * jnp.asarray(ATTN_SCALE, q.dtype)
    zero = jnp.zeros_like(qs)
    return jnp.where(lane < HEAD_DIM, qs, zero), jnp.where(lane < HEAD_DIM, zero, qs)


_NT_DIMS = (((1,), (1,)), ((), ()))


def _causal_mask(tile):
    row = lax.broadcasted_iota(jnp.int32, (tile, tile), 0)
    col = lax.broadcasted_iota(jnp.int32, (tile, tile), 1)
    return row >= col


def _fox_kernel(q_ref, k_ref, v_ref, f_ref, o_ref, m_sc, l_sc, acc_sc, *, tile):
    i = pl.program_id(2)
    qh = _split_heads(q_ref[...])
    q0 = pl.multiple_of(i * tile, tile)
    f_start = f_ref[:, pl.ds(q0, LANES)][:, 0:1]
    _softmax_init(m_sc, l_sc, acc_sc)

    def step(j, masked):
        k0 = pl.multiple_of(j * tile, tile)
        k = k_ref[pl.ds(k0, tile), :]
        v = v_ref[pl.ds(k0, tile), :]
        decay = f_ref[:, pl.ds(k0, tile)] - f_start
        for hd in range(2):
            s = lax.dot_general(qh[hd], k, _NT_DIMS, preferred_element_type=F32)
            s = s - decay[hd:hd + 1, :]
            if masked:
                s = jnp.where(_causal_mask(tile), s, MASK_VALUE)
            _softmax_step(hd, s, v, m_sc, l_sc, acc_sc)

    step(i, True)
    lax.fori_loop(0, i, lambda j, c: (step(j, False), c)[1], 0)
    _softmax_finish(o_ref, l_sc, acc_sc)


def _fox_attention(q, k, v, fcum):
    bsz, seq, _ = q.shape
    tile = min(ATT_TILE, seq)
    assert seq % tile == 0 and tile % LANES == 0
    n_pair = H_FOX // 2
    q_spec = pl.BlockSpec((None, tile, HEAD_PAIR_W), lambda b, p, i: (b, i, p))
    kv_spec = pl.BlockSpec((None, seq, HEAD_PAIR_W), lambda b, p, i: (b, 0, p))
    f_spec = pl.BlockSpec((None, None, 2, seq), lambda b, p, i: (b, p, 0, 0))
    return pl.pallas_call(
        functools.partial(_fox_kernel, tile=tile),
        grid=(bsz, n_pair, seq // tile),
        in_specs=[q_spec, kv_spec, kv_spec, f_spec],
        out_specs=q_spec,
        out_shape=jax.ShapeDtypeStruct((bsz, seq, H_FOX * HEAD_DIM), BF16),
        scratch_shapes=[pltpu.VMEM((2, tile, 1), F32), pltpu.VMEM((2, tile, 1), F32),
                        pltpu.VMEM((2, tile, HEAD_PAIR_W), F32)],
        compiler_params=_params(("arbitrary", "arbitrary", "arbitrary")),
        name="fox_attention",
    )(q, k, v, fcum)


def _t5_bias_tile(dist, rel_ref, head):
    exact = T5_BUCKETS // 2
    d_f = jnp.maximum(dist, 1).astype(F32)
    log_b = exact + (jnp.log(d_f / exact) / math.log(T5_MAX_DIST / exact)
                     * (T5_BUCKETS - exact)).astype(jnp.int32)
    log_b = jnp.minimum(log_b, T5_BUCKETS - 1)
    bucket = jnp.where(dist < exact, dist, log_b)
    out = jnp.zeros(dist.shape, F32)
    for b in range(T5_BUCKETS):
        out = jnp.where(bucket == b, rel_ref[b, head], out)
    return out


def _moba_kernel(rel_ref, q_ref, k_ref, v_ref, o_ref,
                 kaug_sc, kmean_sc, bias_sc, m_sc, l_sc, acc_sc, *, seq):
    blk = MOBA_BLOCK
    n_blk = seq // blk
    p = pl.program_id(1)
    i = pl.program_id(2)

    @pl.when(i == 0)
    def _():
        kaug_sc[:, 0:HEAD_PAIR_W] = k_ref[...]
        r_blk = lax.broadcasted_iota(jnp.int32, (seq, LANES), 0) // blk
        lane = lax.broadcasted_iota(jnp.int32, (seq, LANES), 1)
        kaug_sc[:, HEAD_PAIR_W:] = jnp.where(r_blk == lane, 1.0, 0.0).astype(BF16)
        mrow = lax.broadcasted_iota(jnp.int32, (LANES, seq), 0)
        mcol = lax.broadcasted_iota(jnp.int32, (LANES, seq), 1) // blk
        avg = jnp.where(mrow == mcol, 1.0 / blk, 0.0).astype(BF16)
        kmean_sc[...] = jnp.dot(avg, k_ref[...], preferred_element_type=F32)
        row = lax.broadcasted_iota(jnp.int32, (blk, blk), 0)
        col = lax.broadcasted_iota(jnp.int32, (blk, blk), 1)
        for hd in range(2):
            for delta in range(2):
                dist = jnp.maximum(delta * blk + row - col, 0)
                bias_sc[hd, delta] = _t5_bias_tile(dist, rel_ref, 2 * p + hd)

    qh = _split_heads(q_ref[...])
    lane = lax.broadcasted_iota(jnp.int32, (blk, LANES), 1)
    lane_mask = (lax.broadcasted_iota(jnp.int32, (1, HEAD_PAIR_W), 1) < HEAD_DIM)

    kmean = kmean_sc[...]
    km_hi = kmean.astype(BF16)
    km_lo = (kmean - km_hi.astype(F32)).astype(BF16)
    q_aug = []
    for hd in range(2):
        gate = (lax.dot_general(qh[hd], km_hi, _NT_DIMS, preferred_element_type=F32)
                + lax.dot_general(qh[hd], km_lo, _NT_DIMS, preferred_element_type=F32))
        gate = jnp.where(lane < i, gate, -jnp.inf)
        sel = jnp.zeros(gate.shape, jnp.bool_)
        for _ in range(MOBA_TOPK):
            best = jnp.max(gate, axis=-1, keepdims=True)
            first = jnp.min(jnp.where(gate == best, lane, LANES), axis=-1, keepdims=True)
            pick = (lane == first) & (best > -jnp.inf)
            sel = sel | pick
            gate = jnp.where(pick, -jnp.inf, gate)
        pen = jnp.where(sel, 0.0, MASK_VALUE).astype(BF16)
        pen = jnp.where(lane < n_blk, pen, jnp.zeros_like(pen))
        q_aug.append(jnp.concatenate([qh[hd], pen], axis=-1))

    _softmax_init(m_sc, l_sc, acc_sc)
    causal = _causal_mask(blk)
    far_bias = [rel_ref[T5_BUCKETS - 1, 2 * p + hd] for hd in range(2)]

    q0 = pl.multiple_of(i * blk, blk)
    k_own = k_ref[pl.ds(q0, blk), :]
    v_own = v_ref[pl.ds(q0, blk), :]
    for hd in range(2):
        s = lax.dot_general(qh[hd], k_own, _NT_DIMS, preferred_element_type=F32)
        s = jnp.where(causal, s + bias_sc[hd, 0], MASK_VALUE)
        _softmax_step(hd, s, v_own, m_sc, l_sc, acc_sc)

    def past_step(j, near):
        k0 = pl.multiple_of(j * blk, blk)
        k = kaug_sc[pl.ds(k0, blk), :]
        v = v_ref[pl.ds(k0, blk), :]
        for hd in range(2):
            s = lax.dot_general(q_aug[hd], k, _NT_DIMS, preferred_element_type=F32)
            s = s + (bias_sc[hd, 1] if near else far_bias[hd])
            _softmax_step(hd, s, v, m_sc, l_sc, acc_sc)

    @pl.when(i >= 1)
    def _():
        past_step(i - 1, True)

    lax.fori_loop(0, jnp.maximum(i - 1, 0), lambda j, c: (past_step(j, False), c)[1], 0)
    _softmax_finish(o_ref, l_sc, acc_sc)
    del lane_mask


def _moba_attention(q, k, v, rel_bias):
    bsz, seq, _ = q.shape
    blk = MOBA_BLOCK
    assert seq % blk == 0 and seq // blk <= LANES
    assert blk + 1 >= T5_MAX_DIST
    n_pair = H_MOBA // 2
    off = H_FOX // 2
    q_spec = pl.BlockSpec((None, blk, HEAD_PAIR_W), lambda b, p, i, rel: (b, i, p + off))
    kv_spec = pl.BlockSpec((None, seq, HEAD_PAIR_W), lambda b, p, i, rel: (b, 0, p + off))
    o_spec = pl.BlockSpec((None, blk, HEAD_PAIR_W), lambda b, p, i, rel: (b, i, p))
    return pl.pallas_call(
        functools.partial(_moba_kernel, seq=seq),
        grid_spec=pltpu.PrefetchScalarGridSpec(
            num_scalar_prefetch=1,
            grid=(bsz, n_pair, seq // blk),
            in_specs=[q_spec, kv_spec, kv_spec],
            out_specs=o_spec,
            scratch_shapes=[pltpu.VMEM((seq, 2 * HEAD_PAIR_W), BF16),
                            pltpu.VMEM((LANES, HEAD_PAIR_W), F32),
                            pltpu.VMEM((2, 2, blk, blk), F32),
                            pltpu.VMEM((2, blk, 1), F32), pltpu.VMEM((2, blk, 1), F32),
                            pltpu.VMEM((2, blk, HEAD_PAIR_W), F32)]),
        out_shape=jax.ShapeDtypeStruct((bsz, seq, H_MOBA * HEAD_DIM), BF16),
        compiler_params=_params(("arbitrary", "arbitrary", "arbitrary")),
        name="moba_attention",
    )(rel_bias, q, k, v)


def _attn_out_kernel(h_ref, yc_ref, yd_ref, w_ref, o_ref):
    wc = H_FOX * HEAD_DIM
    o_ref[...] = (h_ref[...]
                  + jnp.dot(yc_ref[...], w_ref[0:wc, :], preferred_element_type=F32)
                  + jnp.dot(yd_ref[...], w_ref[wc:, :], preferred_element_type=F32))


def _attn_out(h, y_c, y_d, w_out):
    bsz, seq, d = h.shape
    tile = min(TOKEN_TILE, seq)
    act_spec = pl.BlockSpec((None, tile, d), lambda b, s: (b, s, 0))
    yc_spec = pl.BlockSpec((None, tile, y_c.shape[-1]), lambda b, s: (b, s, 0))
    yd_spec = pl.BlockSpec((None, tile, y_d.shape[-1]), lambda b, s: (b, s, 0))
    return pl.pallas_call(
        _attn_out_kernel,
        grid=(bsz, seq // tile),
        in_specs=[act_spec, yc_spec, yd_spec, _const_spec(w_out.shape)],
        out_specs=act_spec,
        out_shape=jax.ShapeDtypeStruct(h.shape, F32),
        compiler_params=_params(("arbitrary", "arbitrary")),
        name="attn_out",
    )(h, y_c, y_d, w_out.astype(BF16))


def kernel(x, mix_norm_g, ffn_norm_g, final_norm_g, ev_w_in, ev_conv_w, ev_pool_w, ev_pool_scale,
           ev_w_out, od_w_in, od_b_f, od_w_out, rel_bias, ffn_w_in, ffn_conv_w, ffn_conv_b, ffn_w_out):
    depth = mix_norm_g.shape[0]
    h = x
    for layer in range(depth):
        if layer % 2 == 0:
            e = layer // 2
            h = _even_layer(h, mix_norm_g[layer], ev_w_in[e], ev_conv_w[e], ev_pool_w[e],
                            ev_pool_scale[e], ev_w_out[e])
        else:
            o = layer // 2
            q, k, v, zf = _qkv_proj(h, mix_norm_g[layer], od_w_in[o])
            fcum = _fgate_cumsum(zf, od_b_f[o])
            y_c = _fox_attention(q, k, v, fcum)
            y_d = _moba_attention(q, k, v, rel_bias)
            h = _attn_out(h, y_c, y_d, od_w_out[o])
        h = _ffn_layer(h, ffn_norm_g[layer], ffn_w_in[layer], ffn_conv_w[layer], ffn_conv_b[layer],
                       ffn_w_out[layer], final_norm_g, final_norm=(layer == depth - 1))
    return h
```

```python
import functools
import math

import jax
import jax.numpy as jnp
from jax import lax
from jax.experimental import pallas as pl
from jax.experimental.pallas import tpu as pltpu

F32 = jnp.float32
BF16 = jnp.bfloat16

D_MODEL = 1024
HEAD_DIM = 64
CONV_W = 3
A_W = D_MODEL // 2
B_W = D_MODEL - A_W
POOL_WINDOWS = (2, 4, 8, 16)
POOL_G = B_W // len(POOL_WINDOWS)
H_FOX = D_MODEL // (2 * HEAD_DIM)
H_MOBA = D_MODEL // (2 * HEAD_DIM)
ATT_W = (H_FOX + H_MOBA) * HEAD_DIM
ATTN_SCALE = HEAD_DIM ** -0.5
MOBA_BLOCK = 256
MOBA_TOPK = 3
T5_BUCKETS = 32
T5_MAX_DIST = 128
D_FF = 2816
RMS_EPS = 1e-6

LANES = 128
SUBLANES = 8
BF16_ROWS = 16
VMEM_LIMIT_BYTES = 56 * 1024 * 1024

HEAD_PAIR_W = 2 * HEAD_DIM
N_PAIR = H_FOX // 2
MASK_VALUE = -1e30
F_PIECES = 3

TOKEN_TILE = 512
FFN_CHUNK = D_FF // 2
ATT_TILE = MOBA_BLOCK
POOL_HALO = 32


def _rmsnorm(x, g):
    ms = jnp.mean(x * x, axis=-1, keepdims=True)
    return x * lax.rsqrt(ms + RMS_EPS) * g


def _const_spec(shape):
    nd = len(shape)
    return pl.BlockSpec(shape, lambda *_: (0,) * nd, pipeline_mode=pl.Buffered(1))


def _params(semantics):
    return pltpu.CompilerParams(dimension_semantics=semantics, vmem_limit_bytes=VMEM_LIMIT_BYTES)


def _even_kernel(h_ref, g_ref, win_ref, cw_ref, pw_ref, ps_ref, wout_ref, o_ref,
                 cv_sc, u_sc, lvl_sc, tailcv_sc, tailu_sc, *, tile):
    s = pl.program_id(1)
    halo = POOL_HALO

    @pl.when(s == 0)
    def _():
        tailcv_sc[...] = jnp.zeros_like(tailcv_sc)
        tailu_sc[...] = jnp.zeros_like(tailu_sc)

    h = h_ref[...]
    hn = _rmsnorm(h, g_ref[...]).astype(BF16)
    z = jnp.dot(hn, win_ref[...], preferred_element_type=F32)
    gate_b = z[:, :A_W]
    cv = z[:, A_W:2 * A_W] * z[:, 2 * A_W:3 * A_W]
    pool_in = z[:, 3 * A_W:]

    cv_sc[0:SUBLANES, :] = tailcv_sc[...]
    cv_sc[SUBLANES:SUBLANES + tile, :] = cv
    tailcv_sc[...] = cv[tile - SUBLANES:, :]
    conv = (cw_ref[0:1, :] * cv_sc[SUBLANES - 2:SUBLANES - 2 + tile, :]
            + cw_ref[1:2, :] * cv_sc[SUBLANES - 1:SUBLANES - 1 + tile, :]
            + cw_ref[2:3, :] * cv)
    y_a = gate_b * conv

    u_sc[0:halo, :] = tailu_sc[...]
    u_sc[halo:halo + tile, :] = pool_in
    tailu_sc[...] = pool_in[tile - halo:, :]
    pos = s * tile + lax.broadcasted_iota(jnp.int32, (tile, 1), 0)
    ys = [y_a.astype(BF16)]
    for g, w in enumerate(POOL_WINDOWS):
        cols = slice(g * POOL_G, (g + 1) * POOL_G)
        n_lvl = g + 1
        src, src_cols = u_sc, cols
        for lvl in range(1, n_lvl):
            sh = 2 ** (lvl - 1)
            lo = SUBLANES * lvl
            dst = lvl_sc.at[lvl % 2]
            dst[lo:halo + tile, :] = (src[lo:halo + tile, src_cols]
                                      + src[lo - sh:halo + tile - sh, src_cols])
            src, src_cols = dst, slice(None)
        sh = 2 ** (n_lvl - 1)
        wsum = src[halo:halo + tile, src_cols] + src[halo - sh:halo + tile - sh, src_cols]
        cnt = jnp.minimum(pos + 1, w).astype(F32)
        p = wsum / cnt - pool_in[:, cols]
        yb = jnp.dot(p.astype(BF16), pw_ref[g], preferred_element_type=F32) * ps_ref[:, cols]
        ys.append(yb.astype(BF16))
    y = jnp.concatenate(ys, axis=-1)
    o_ref[...] = h + jnp.dot(y, wout_ref[...], preferred_element_type=F32)


def _even_layer(h, g, w_in, conv_w, pool_w, pool_scale, w_out):
    bsz, seq, d = h.shape
    tile = min(TOKEN_TILE, seq)
    assert seq % tile == 0 and tile >= POOL_HALO
    act_spec = pl.BlockSpec((None, tile, d), lambda b, s: (b, s, 0))
    return pl.pallas_call(
        functools.partial(_even_kernel, tile=tile),
        grid=(bsz, seq // tile),
        in_specs=[act_spec, _const_spec((1, d)), _const_spec(w_in.shape), _const_spec(conv_w.shape),
                  _const_spec(pool_w.shape), _const_spec((1, B_W)), _const_spec(w_out.shape)],
        out_specs=act_spec,
        out_shape=jax.ShapeDtypeStruct(h.shape, F32),
        scratch_shapes=[pltpu.VMEM((SUBLANES + tile, A_W), F32),
                        pltpu.VMEM((POOL_HALO + tile, B_W), F32),
                        pltpu.VMEM((2, POOL_HALO + tile, POOL_G), F32),
                        pltpu.VMEM((SUBLANES, A_W), F32),
                        pltpu.VMEM((POOL_HALO, B_W), F32)],
        compiler_params=_params(("arbitrary", "arbitrary")),
        name="even_mixer",
    )(h, g.reshape(1, d), w_in.astype(BF16), conv_w, pool_w.astype(BF16),
      pool_scale.reshape(1, B_W), w_out.astype(BF16))


def _ffn_kernel(h_ref, g_ref, win_ref, cw_ref, cb_ref, wout_ref, fg_ref, o_ref, ext_sc, tail_sc,
                *, tile, final_norm):
    s = pl.program_id(1)

    @pl.when(s == 0)
    def _():
        tail_sc[...] = jnp.zeros_like(tail_sc)

    h = h_ref[...]
    hn = _rmsnorm(h, g_ref[...]).astype(BF16)
    acc = h
    for c in range(D_FF // FFN_CHUNK):
        cols = slice(c * FFN_CHUNK, (c + 1) * FFN_CHUNK)
        gcols = slice(D_FF + c * FFN_CHUNK, D_FF + (c + 1) * FFN_CHUNK)
        u = jnp.dot(hn, win_ref[:, cols], preferred_element_type=F32)
        gt = jnp.dot(hn, win_ref[:, gcols], preferred_element_type=F32)
        ext_sc[0:SUBLANES, :] = tail_sc[:, cols]
        ext_sc[SUBLANES:SUBLANES + tile, :] = u
        tail_sc[:, cols] = u[tile - SUBLANES:, :]
        a = (cw_ref[0:1, cols] * ext_sc[SUBLANES - 2:SUBLANES - 2 + tile, :]
             + cw_ref[1:2, cols] * ext_sc[SUBLANES - 1:SUBLANES - 1 + tile, :]
             + cw_ref[2:3, cols] * u
             + cb_ref[:, cols])
        act = a * (1.0 / (1.0 + jnp.exp(-a))) * gt
        acc = acc + jnp.dot(act.astype(BF16), wout_ref[cols, :], preferred_element_type=F32)
    if final_norm:
        acc = _rmsnorm(acc, fg_ref[...])
    o_ref[...] = acc


def _ffn_layer(h, g, w_in, conv_w, conv_b, w_out, final_g, final_norm):
    bsz, seq, d = h.shape
    tile = min(TOKEN_TILE, seq)
    assert seq % tile == 0
    act_spec = pl.BlockSpec((None, tile, d), lambda b, s: (b, s, 0))
    return pl.pallas_call(
        functools.partial(_ffn_kernel, tile=tile, final_norm=final_norm),
        grid=(bsz, seq // tile),
        in_specs=[act_spec, _const_spec((1, d)), _const_spec(w_in.shape), _const_spec(conv_w.shape),
                  _const_spec((1, D_FF)), _const_spec(w_out.shape), _const_spec((1, d))],
        out_specs=act_spec,
        out_shape=jax.ShapeDtypeStruct(h.shape, F32),
        scratch_shapes=[pltpu.VMEM((SUBLANES + tile, FFN_CHUNK), F32),
                        pltpu.VMEM((SUBLANES, D_FF), F32)],
        compiler_params=_params(("arbitrary", "arbitrary")),
        name="conv_ffn",
    )(h, g.reshape(1, d), w_in.astype(BF16), conv_w, conv_b.reshape(1, D_FF), w_out.astype(BF16),
      final_g.reshape(1, d))


_NT_DIMS = (((1,), (1,)), ((), ()))


def _qkv_kernel(h_ref, g_ref, wqt_ref, wk_ref, wvt_ref, wf_ref, qt_ref, k_ref, vt_ref, zf_ref):
    hn = _rmsnorm(h_ref[...], g_ref[...]).astype(BF16)
    qt = lax.dot_general(wqt_ref[...], hn, _NT_DIMS, preferred_element_type=F32)
    qt_ref[...] = (qt * ATTN_SCALE).astype(BF16)
    k_ref[...] = jnp.dot(hn, wk_ref[...], preferred_element_type=F32).astype(BF16)
    vt_ref[...] = lax.dot_general(wvt_ref[...], hn, _NT_DIMS, preferred_element_type=F32).astype(BF16)
    zf_ref[...] = jnp.dot(hn, wf_ref[...], preferred_element_type=F32)


def _qkv_proj(h, g, w_in):
    bsz, seq, d = h.shape
    tile = min(TOKEN_TILE, seq)
    assert seq % tile == 0
    w_qt = w_in[:, :ATT_W].T.astype(BF16)
    w_k = w_in[:, ATT_W:2 * ATT_W].astype(BF16)
    w_vt = w_in[:, 2 * ATT_W:3 * ATT_W].T.astype(BF16)
    w_f = jnp.pad(w_in[:, 3 * ATT_W:], ((0, 0), (0, LANES - H_FOX))).astype(BF16)
    act_spec = pl.BlockSpec((None, tile, d), lambda b, s: (b, s, 0))
    tok_major = pl.BlockSpec((None, tile, ATT_W), lambda b, s: (b, s, 0))
    feat_major = pl.BlockSpec((None, ATT_W, tile), lambda b, s: (b, 0, s))
    zf_spec = pl.BlockSpec((None, tile, LANES), lambda b, s: (b, s, 0))
    return pl.pallas_call(
        _qkv_kernel,
        grid=(bsz, seq // tile),
        in_specs=[act_spec, _const_spec((1, d)), _const_spec(w_qt.shape), _const_spec(w_k.shape),
                  _const_spec(w_vt.shape), _const_spec(w_f.shape)],
        out_specs=[feat_major, tok_major, feat_major, zf_spec],
        out_shape=[jax.ShapeDtypeStruct((bsz, ATT_W, seq), BF16),
                   jax.ShapeDtypeStruct((bsz, seq, ATT_W), BF16),
                   jax.ShapeDtypeStruct((bsz, ATT_W, seq), BF16),
                   jax.ShapeDtypeStruct((bsz, seq, LANES), F32)],
        compiler_params=_params(("arbitrary", "arbitrary")),
        name="qkv_proj",
    )(h, g.reshape(1, d), w_qt, w_k, w_vt, w_f)


def _fgate_kernel(zf_ref, bf_ref, o_ref, *, seq):
    z = zf_ref[...] + bf_ref[...]
    x = -(jnp.maximum(-z, 0.0) + jnp.log1p(jnp.exp(-jnp.abs(z))))
    row = lax.broadcasted_iota(jnp.int32, (seq, LANES), 0)
    sh = 1
    while sh < seq:
        x = x + jnp.where(row >= sh, pltpu.roll(x, sh, 0), 0.0)
        sh *= 2
    pieces = []
    for _ in range(F_PIECES):
        piece = x.astype(BF16)
        pieces.append(piece)
        x = x - piece.astype(F32)
    src = lax.broadcasted_iota(jnp.int32, (F_PIECES * LANES, N_PAIR * LANES), 0)
    dst = lax.broadcasted_iota(jnp.int32, (F_PIECES * LANES, N_PAIR * LANES), 1)
    piece_id, head = src // LANES, src % LANES
    pair, slot = dst // LANES, dst % LANES
    hit = (head // 2 == pair) & (slot == F_PIECES * (head % 2) + piece_id) & (head < H_FOX)
    place = jnp.where(hit, 1.0, 0.0).astype(BF16)
    o_ref[...] = jnp.dot(jnp.concatenate(pieces, axis=-1), place,
                         preferred_element_type=F32).astype(BF16)


def _fgate_cumsum(zf, b_f):
    bsz, seq, _ = zf.shape
    return pl.pallas_call(
        functools.partial(_fgate_kernel, seq=seq),
        grid=(bsz,),
        in_specs=[pl.BlockSpec((None, seq, LANES), lambda b: (b, 0, 0)), _const_spec((1, LANES))],
        out_specs=pl.BlockSpec((None, seq, N_PAIR * LANES), lambda b: (b, 0, 0)),
        out_shape=jax.ShapeDtypeStruct((bsz, seq, N_PAIR * LANES), BF16),
        compiler_params=_params(("arbitrary",)),
        name="fgate_cumsum",
    )(zf, jnp.pad(b_f, (0, LANES - H_FOX)).reshape(1, LANES))


ACC_ROWS = HEAD_DIM + BF16_ROWS


def _softmax_init(m_sc, acc_sc):
    m_sc[...] = jnp.full_like(m_sc, MASK_VALUE)
    acc_sc[...] = jnp.zeros_like(acc_sc)


def _softmax_tiles(s_tiles, v_tiles, m_sc, acc_sc):
    p_tiles, alphas = [], []
    for h, s_t in enumerate(s_tiles):
        m_prev = m_sc[h]
        m_new = jnp.maximum(m_prev, jnp.max(s_t, axis=0, keepdims=True))
        alphas.append(jnp.exp(m_prev - m_new))
        p_tiles.append(jnp.exp(s_t - m_new).astype(BF16))
        m_sc[h] = m_new
    for h, (p_t, v_aug) in enumerate(zip(p_tiles, v_tiles)):
        acc_sc[h] = alphas[h] * acc_sc[h] + jnp.dot(v_aug, p_t, preferred_element_type=F32)


def _value_tile(vt_ref, h, k0, tk):
    v = vt_ref[h * HEAD_DIM:(h + 1) * HEAD_DIM, pl.ds(k0, tk)]
    return jnp.concatenate([v, jnp.ones((BF16_ROWS, tk), BF16)], axis=0)


def _softmax_finish(o_ref, acc_sc, n_heads):
    for pair in range(n_heads // 2):
        halves = []
        for h in (2 * pair, 2 * pair + 1):
            acc = acc_sc[h]
            halves.append(acc[0:HEAD_DIM, :] / acc[HEAD_DIM:HEAD_DIM + 1, :])
        out_t = jnp.concatenate(halves, axis=0)
        o_ref[:, pair * HEAD_PAIR_W:(pair + 1) * HEAD_PAIR_W] = jnp.transpose(out_t).astype(o_ref.dtype)


def _head_rows(qt_ref, h):
    pair, e = divmod(h, 2)
    q = qt_ref[pair * HEAD_PAIR_W:(pair + 1) * HEAD_PAIR_W, :]
    row = lax.broadcasted_iota(jnp.int32, q.shape, 0)
    mine = (row >= e * HEAD_DIM) & (row < (e + 1) * HEAD_DIM)
    return jnp.where(mine, q, jnp.zeros_like(q))


def _key_le_query(tile):
    key = lax.broadcasted_iota(jnp.int32, (tile, tile), 0)
    query = lax.broadcasted_iota(jnp.int32, (tile, tile), 1)
    return key <= query


def _fox_kernel(qt_ref, k_ref, f_ref, vt_ref, o_ref, kaug_sc, m_sc, acc_sc, *, tile):
    i = pl.program_id(1)

    @pl.when(i == 0)
    def _():
        for pair in range(N_PAIR):
            kaug_sc[pair, :, 0:HEAD_PAIR_W] = k_ref[:, pair * HEAD_PAIR_W:(pair + 1) * HEAD_PAIR_W]
            kaug_sc[pair, :, HEAD_PAIR_W:] = f_ref[:, pair * LANES:(pair + 1) * LANES]

    row = lax.broadcasted_iota(jnp.int32, (LANES, tile), 0)
    q_aug = []
    for h in range(H_FOX):
        e = h % 2
        minus_one = (row >= F_PIECES * e) & (row < F_PIECES * (e + 1))
        sel = jnp.where(minus_one, -1.0, 0.0).astype(BF16)
        q_aug.append(jnp.concatenate([_head_rows(qt_ref, h), sel], axis=0))

    _softmax_init(m_sc, acc_sc)

    def step(j, masked):
        k0 = pl.multiple_of(j * tile, tile)
        s_tiles = []
        for h in range(H_FOX):
            k = kaug_sc[h // 2, pl.ds(k0, tile), :]
            s_t = jnp.dot(k, q_aug[h], preferred_element_type=F32)
            if masked:
                s_t = jnp.where(_key_le_query(tile), s_t, MASK_VALUE)
            s_tiles.append(s_t)
        v_tiles = [_value_tile(vt_ref, h, k0, tile) for h in range(H_FOX)]
        _softmax_tiles(s_tiles, v_tiles, m_sc, acc_sc)

    step(i, True)
    lax.fori_loop(0, i, lambda j, c: (step(j, False), c)[1], 0)
    _softmax_finish(o_ref, acc_sc, H_FOX)


def _fox_attention(qt, k, vt, faug):
    bsz, seq, _ = k.shape
    tile = min(ATT_TILE, seq)
    assert seq % tile == 0 and tile % LANES == 0
    width = H_FOX * HEAD_DIM
    return pl.pallas_call(
        functools.partial(_fox_kernel, tile=tile),
        grid=(bsz, seq // tile),
        in_specs=[pl.BlockSpec((None, width, tile), lambda b, i: (b, 0, i)),
                  pl.BlockSpec((None, seq, width), lambda b, i: (b, 0, 0)),
                  pl.BlockSpec((None, seq, N_PAIR * LANES), lambda b, i: (b, 0, 0)),
                  pl.BlockSpec((None, width, seq), lambda b, i: (b, 0, 0))],
        out_specs=pl.BlockSpec((None, tile, width), lambda b, i: (b, i, 0)),
        out_shape=jax.ShapeDtypeStruct((bsz, seq, width), BF16),
        scratch_shapes=[pltpu.VMEM((N_PAIR, seq, 2 * HEAD_PAIR_W), BF16),
                        pltpu.VMEM((H_FOX, 1, tile), F32),
                        pltpu.VMEM((H_FOX, ACC_ROWS, tile), F32)],
        compiler_params=_params(("arbitrary", "arbitrary")),
        name="fox_attention",
    )(qt, k, faug, vt)


def _t5_bias_tile(dist, rel_ref, head):
    exact = T5_BUCKETS // 2
    d_f = jnp.maximum(dist, 1).astype(F32)
    log_b = exact + (jnp.log(d_f / exact) / math.log(T5_MAX_DIST / exact)
                     * (T5_BUCKETS - exact)).astype(jnp.int32)
    log_b = jnp.minimum(log_b, T5_BUCKETS - 1)
    bucket = jnp.where(dist < exact, dist, log_b)
    out = jnp.zeros(dist.shape, F32)
    for b in range(T5_BUCKETS):
        out = jnp.where(bucket == b, rel_ref[b, head], out)
    return out


def _moba_kernel(rel_ref, qt_ref, k_ref, vt_ref, o_ref,
                 kmean_sc, bias_sc, pen_sc, m_sc, acc_sc, *, seq, blk_rows):
    blk = MOBA_BLOCK
    b = pl.program_id(0)
    i = pl.program_id(1)

    @pl.when((b == 0) & (i == 0))
    def _():
        key = lax.broadcasted_iota(jnp.int32, (blk, blk), 0)
        query = lax.broadcasted_iota(jnp.int32, (blk, blk), 1)
        for h in range(H_MOBA):
            for delta in range(2):
                dist = jnp.maximum(delta * blk + query - key, 0)
                bias_sc[h, delta] = _t5_bias_tile(dist, rel_ref, h)

    @pl.when(i == 0)
    def _():
        mrow = lax.broadcasted_iota(jnp.int32, (blk_rows, seq), 0)
        mcol = lax.broadcasted_iota(jnp.int32, (blk_rows, seq), 1) // blk
        avg = jnp.where(mrow == mcol, 1.0 / blk, 0.0).astype(BF16)
        kmean_sc[...] = jnp.dot(avg, k_ref[...], preferred_element_type=F32)

    qh = [_head_rows(qt_ref, h) for h in range(H_MOBA)]

    blk_id = lax.broadcasted_iota(jnp.int32, (blk_rows, blk), 0)
    for h in range(H_MOBA):
        pair = h // 2
        kmean = kmean_sc[:, pair * HEAD_PAIR_W:(pair + 1) * HEAD_PAIR_W]
        km_hi = kmean.astype(BF16)
        km_lo = (kmean - km_hi.astype(F32)).astype(BF16)
        gate = (jnp.dot(km_hi, qh[h], preferred_element_type=F32)
                + jnp.dot(km_lo, qh[h], preferred_element_type=F32))
        gate = jnp.where(blk_id < i, gate, -jnp.inf)
        sel = jnp.zeros(gate.shape, jnp.bool_)
        for _ in range(MOBA_TOPK):
            best = jnp.max(gate, axis=0, keepdims=True)
            first = jnp.min(jnp.where(gate == best, blk_id, blk_rows), axis=0, keepdims=True)
            pick = (blk_id == first) & (best > -jnp.inf)
            sel = sel | pick
            gate = jnp.where(pick, -jnp.inf, gate)
        pen = jnp.where(sel, 0.0, MASK_VALUE)
        pen_sc[h, 0] = pen
        pen_sc[h, 1] = pen + rel_ref[T5_BUCKETS - 1, h]

    _softmax_init(m_sc, acc_sc)

    def scores(h, k0):
        pair = h // 2
        k = k_ref[pl.ds(k0, blk), pair * HEAD_PAIR_W:(pair + 1) * HEAD_PAIR_W]
        return jnp.dot(k, qh[h], preferred_element_type=F32)

    q0 = pl.multiple_of(i * blk, blk)
    allowed = _key_le_query(blk)
    s_tiles = [jnp.where(allowed, scores(h, q0) + bias_sc[h, 0], MASK_VALUE) for h in range(H_MOBA)]
    v_tiles = [_value_tile(vt_ref, h, q0, blk) for h in range(H_MOBA)]
    _softmax_tiles(s_tiles, v_tiles, m_sc, acc_sc)

    def past_step(j, near):
        k0 = pl.multiple_of(j * blk, blk)
        s_tiles = []
        for h in range(H_MOBA):
            s_t = scores(h, k0)
            if near:
                s_t = s_t + bias_sc[h, 1] + pen_sc[h, 0, pl.ds(j, 1), :]
            else:
                s_t = s_t + pen_sc[h, 1, pl.ds(j, 1), :]
            s_tiles.append(s_t)
        v_tiles = [_value_tile(vt_ref, h, k0, blk) for h in range(H_MOBA)]
        _softmax_tiles(s_tiles, v_tiles, m_sc, acc_sc)

    @pl.when(i >= 1)
    def _():
        past_step(i - 1, True)

    lax.fori_loop(0, jnp.maximum(i - 1, 0), lambda j, c: (past_step(j, False), c)[1], 0)
    _softmax_finish(o_ref, acc_sc, H_MOBA)


def _moba_attention(qt, k, vt, rel_bias):
    bsz, seq, _ = k.shape
    blk = MOBA_BLOCK
    assert seq % blk == 0
    assert blk + 1 >= T5_MAX_DIST
    blk_rows = -(-(seq // blk) // BF16_ROWS) * BF16_ROWS
    width = H_MOBA * HEAD_DIM
    off = (H_FOX * HEAD_DIM) // width
    return pl.pallas_call(
        functools.partial(_moba_kernel, seq=seq, blk_rows=blk_rows),
        grid_spec=pltpu.PrefetchScalarGridSpec(
            num_scalar_prefetch=1,
            grid=(bsz, seq // blk),
            in_specs=[pl.BlockSpec((None, width, blk), lambda b, i, rel: (b, off, i)),
                      pl.BlockSpec((None, seq, width), lambda b, i, rel: (b, 0, off)),
                      pl.BlockSpec((None, width, seq), lambda b, i, rel: (b, off, 0))],
            out_specs=pl.BlockSpec((None, blk, width), lambda b, i, rel: (b, i, 0)),
            scratch_shapes=[pltpu.VMEM((blk_rows, width), F32),
                            pltpu.VMEM((H_MOBA, 2, blk, blk), F32),
                            pltpu.VMEM((H_MOBA, 2, blk_rows, blk), F32),
                            pltpu.VMEM((H_MOBA, 1, blk), F32),
                            pltpu.VMEM((H_MOBA, ACC_ROWS, blk), F32)]),
        out_shape=jax.ShapeDtypeStruct((bsz, seq, width), BF16),
        compiler_params=_params(("arbitrary", "arbitrary")),
        name="moba_attention",
    )(rel_bias, qt, k, vt)


def _attn_out_kernel(h_ref, yc_ref, yd_ref, w_ref, o_ref):
    wc = H_FOX * HEAD_DIM
    o_ref[...] = (h_ref[...]
                  + jnp.dot(yc_ref[...], w_ref[0:wc, :], preferred_element_type=F32)
                  + jnp.dot(yd_ref[...], w_ref[wc:, :], preferred_element_type=F32))


def _attn_out(h, y_c, y_d, w_out):
    bsz, seq, d = h.shape
    tile = min(TOKEN_TILE, seq)
    act_spec = pl.BlockSpec((None, tile, d), lambda b, s: (b, s, 0))
    yc_spec = pl.BlockSpec((None, tile, y_c.shape[-1]), lambda b, s: (b, s, 0))
    yd_spec = pl.BlockSpec((None, tile, y_d.shape[-1]), lambda b, s: (b, s, 0))
    return pl.pallas_call(
        _attn_out_kernel,
        grid=(bsz, seq // tile),
        in_specs=[act_spec, yc_spec, yd_spec, _const_spec(w_out.shape)],
        out_specs=act_spec,
        out_shape=jax.ShapeDtypeStruct(h.shape, F32),
        compiler_params=_params(("arbitrary", "arbitrary")),
        name="attn_out",
    )(h, y_c, y_d, w_out.astype(BF16))


def kernel(x, mix_norm_g, ffn_norm_g, final_norm_g, ev_w_in, ev_conv_w, ev_pool_w, ev_pool_scale,
           ev_w_out, od_w_in, od_b_f, od_w_out, rel_bias, ffn_w_in, ffn_conv_w, ffn_conv_b, ffn_w_out):
    depth = mix_norm_g.shape[0]
    h = x
    for layer in range(depth):
        if layer % 2 == 0:
            e = layer // 2
            h = _even_layer(h, mix_norm_g[layer], ev_w_in[e], ev_conv_w[e], ev_pool_w[e],
                            ev_pool_scale[e], ev_w_out[e])
        else:
            o = layer // 2
            qt, k, vt, zf = _qkv_proj(h, mix_norm_g[layer], od_w_in[o])
            faug = _fgate_cumsum(zf, od_b_f[o])
            y_c = _fox_attention(qt, k, vt, faug)
            y_d = _moba_attention(qt, k, vt, rel_bias)
            h = _attn_out(h, y_c, y_d, od_w_out[o])
        h = _ffn_layer(h, ffn_norm_g[layer], ffn_w_in[layer], ffn_conv_w[layer], ffn_conv_b[layer],
                       ffn_w_out[layer], final_norm_g, final_norm=(layer == depth - 1))
    return h
```

```python
import functools
import math

import jax
import jax.numpy as jnp
from jax import lax
from jax.experimental import pallas as pl
from jax.experimental.pallas import tpu as pltpu

F32 = jnp.float32
BF16 = jnp.bfloat16

D_MODEL = 1024
HEAD_DIM = 64
CONV_W = 3
A_W = D_MODEL // 2
B_W = D_MODEL - A_W
POOL_WINDOWS = (2, 4, 8, 16)
POOL_G = B_W // len(POOL_WINDOWS)
H_FOX = D_MODEL // (2 * HEAD_DIM)
H_MOBA = D_MODEL // (2 * HEAD_DIM)
ATT_W = (H_FOX + H_MOBA) * HEAD_DIM
ATTN_SCALE = HEAD_DIM ** -0.5
MOBA_BLOCK = 256
MOBA_TOPK = 3
T5_BUCKETS = 32
T5_MAX_DIST = 128
D_FF = 2816
RMS_EPS = 1e-6

LANES = 128
SUBLANES = 8
BF16_ROWS = 16
VMEM_LIMIT_BYTES = 56 * 1024 * 1024

HEAD_PAIR_W = 2 * HEAD_DIM
N_PAIR = H_FOX // 2
MASK_VALUE = -1e30
LOG2E = math.log2(math.e)
F_PIECES = 3

TOKEN_TILE = 512
FFN_CHUNK = D_FF
ATT_TILE = MOBA_BLOCK
POOL_HALO = 32


def _rmsnorm(x, g):
    ms = jnp.mean(x * x, axis=-1, keepdims=True)
    return x * lax.rsqrt(ms + RMS_EPS) * g


def _const_spec(shape):
    nd = len(shape)
    return pl.BlockSpec(shape, lambda *_: (0,) * nd, pipeline_mode=pl.Buffered(1))


def _params(semantics):
    return pltpu.CompilerParams(dimension_semantics=semantics, vmem_limit_bytes=VMEM_LIMIT_BYTES)


def _even_kernel(h_ref, g_ref, win_ref, cw_ref, pw_ref, ps_ref, wout_ref, o_ref,
                 cv_sc, u_sc, lvl_sc, tailcv_sc, tailu_sc, *, tile):
    s = pl.program_id(1)
    halo = POOL_HALO

    @pl.when(s == 0)
    def _():
        tailcv_sc[...] = jnp.zeros_like(tailcv_sc)
        tailu_sc[...] = jnp.zeros_like(tailu_sc)

    h = h_ref[...]
    hn = _rmsnorm(h, g_ref[...]).astype(BF16)
    z = jnp.dot(hn, win_ref[...], preferred_element_type=F32)
    gate_b = z[:, :A_W]
    cv = z[:, A_W:2 * A_W] * z[:, 2 * A_W:3 * A_W]
    pool_in = z[:, 3 * A_W:]

    cv_sc[0:SUBLANES, :] = tailcv_sc[...]
    cv_sc[SUBLANES:SUBLANES + tile, :] = cv
    tailcv_sc[...] = cv[tile - SUBLANES:, :]
    conv = (cw_ref[0:1, :] * cv_sc[SUBLANES - 2:SUBLANES - 2 + tile, :]
            + cw_ref[1:2, :] * cv_sc[SUBLANES - 1:SUBLANES - 1 + tile, :]
            + cw_ref[2:3, :] * cv)
    y_a = gate_b * conv

    u_sc[0:halo, :] = tailu_sc[...]
    u_sc[halo:halo + tile, :] = pool_in
    tailu_sc[...] = pool_in[tile - halo:, :]
    pos = s * tile + lax.broadcasted_iota(jnp.int32, (tile, 1), 0)
    ys = [y_a.astype(BF16)]
    for g, w in enumerate(POOL_WINDOWS):
        cols = slice(g * POOL_G, (g + 1) * POOL_G)
        n_lvl = g + 1
        src, src_cols = u_sc, cols
        for lvl in range(1, n_lvl):
            sh = 2 ** (lvl - 1)
            lo = SUBLANES * lvl
            dst = lvl_sc.at[lvl % 2]
            dst[lo:halo + tile, :] = (src[lo:halo + tile, src_cols]
                                      + src[lo - sh:halo + tile - sh, src_cols])
            src, src_cols = dst, slice(None)
        sh = 2 ** (n_lvl - 1)
        wsum = src[halo:halo + tile, src_cols] + src[halo - sh:halo + tile - sh, src_cols]
        cnt = jnp.minimum(pos + 1, w).astype(F32)
        p = wsum / cnt - pool_in[:, cols]
        yb = jnp.dot(p.astype(BF16), pw_ref[g], preferred_element_type=F32) * ps_ref[:, cols]
        ys.append(yb.astype(BF16))
    y = jnp.concatenate(ys, axis=-1)
    o_ref[...] = h + jnp.dot(y, wout_ref[...], preferred_element_type=F32)


def _even_layer(h, g, w_in, conv_w, pool_w, pool_scale, w_out):
    bsz, seq, d = h.shape
    tile = min(TOKEN_TILE, seq)
    assert seq % tile == 0 and tile >= POOL_HALO
    act_spec = pl.BlockSpec((None, tile, d), lambda b, s: (b, s, 0))
    return pl.pallas_call(
        functools.partial(_even_kernel, tile=tile),
        grid=(bsz, seq // tile),
        in_specs=[act_spec, _const_spec((1, d)), _const_spec(w_in.shape), _const_spec(conv_w.shape),
                  _const_spec(pool_w.shape), _const_spec((1, B_W)), _const_spec(w_out.shape)],
        out_specs=act_spec,
        out_shape=jax.ShapeDtypeStruct(h.shape, F32),
        scratch_shapes=[pltpu.VMEM((SUBLANES + tile, A_W), F32),
                        pltpu.VMEM((POOL_HALO + tile, B_W), F32),
                        pltpu.VMEM((2, POOL_HALO + tile, POOL_G), F32),
                        pltpu.VMEM((SUBLANES, A_W), F32),
                        pltpu.VMEM((POOL_HALO, B_W), F32)],
        compiler_params=_params(("arbitrary", "arbitrary")),
        name="even_mixer",
    )(h, g.reshape(1, d), w_in.astype(BF16), conv_w, pool_w.astype(BF16),
      pool_scale.reshape(1, B_W), w_out.astype(BF16))


def _ffn_kernel(h_ref, g_ref, win_ref, cw_ref, cb_ref, wout_ref, fg_ref, o_ref, ext_sc, tail_sc,
                *, tile, final_norm):
    s = pl.program_id(1)

    @pl.when(s == 0)
    def _():
        tail_sc[...] = jnp.zeros_like(tail_sc)

    h = h_ref[...]
    hn = _rmsnorm(h, g_ref[...]).astype(BF16)
    acc = h
    for c in range(D_FF // FFN_CHUNK):
        cols = slice(c * FFN_CHUNK, (c + 1) * FFN_CHUNK)
        gcols = slice(D_FF + c * FFN_CHUNK, D_FF + (c + 1) * FFN_CHUNK)
        u = jnp.dot(hn, win_ref[:, cols], preferred_element_type=F32)
        gt = jnp.dot(hn, win_ref[:, gcols], preferred_element_type=F32)
        ext_sc[0:SUBLANES, :] = tail_sc[:, cols]
        ext_sc[SUBLANES:SUBLANES + tile, :] = u
        tail_sc[:, cols] = u[tile - SUBLANES:, :]
        a = (cw_ref[0:1, cols] * ext_sc[SUBLANES - 2:SUBLANES - 2 + tile, :]
             + cw_ref[1:2, cols] * ext_sc[SUBLANES - 1:SUBLANES - 1 + tile, :]
             + cw_ref[2:3, cols] * u
             + cb_ref[:, cols])
        act = a * (1.0 / (1.0 + jnp.exp(-a))) * gt
        acc = acc + jnp.dot(act.astype(BF16), wout_ref[cols, :], preferred_element_type=F32)
    if final_norm:
        acc = _rmsnorm(acc, fg_ref[...])
    o_ref[...] = acc


def _ffn_layer(h, g, w_in, conv_w, conv_b, w_out, final_g, final_norm):
    bsz, seq, d = h.shape
    tile = min(TOKEN_TILE, seq)
    assert seq % tile == 0
    act_spec = pl.BlockSpec((None, tile, d), lambda b, s: (b, s, 0))
    return pl.pallas_call(
        functools.partial(_ffn_kernel, tile=tile, final_norm=final_norm),
        grid=(bsz, seq // tile),
        in_specs=[act_spec, _const_spec((1, d)), _const_spec(w_in.shape), _const_spec(conv_w.shape),
                  _const_spec((1, D_FF)), _const_spec(w_out.shape), _const_spec((1, d))],
        out_specs=act_spec,
        out_shape=jax.ShapeDtypeStruct(h.shape, F32),
        scratch_shapes=[pltpu.VMEM((SUBLANES + tile, FFN_CHUNK), F32),
                        pltpu.VMEM((SUBLANES, D_FF), F32)],
        compiler_params=_params(("arbitrary", "arbitrary")),
        name="conv_ffn",
    )(h, g.reshape(1, d), w_in.astype(BF16), conv_w, conv_b.reshape(1, D_FF), w_out.astype(BF16),
      final_g.reshape(1, d))


_NT_DIMS = (((1,), (1,)), ((), ()))


def _qkv_kernel(h_ref, g_ref, wqt_ref, wk_ref, wvt_ref, wf_ref, qt_ref, k_ref, vt_ref, zf_ref):
    hn = _rmsnorm(h_ref[...], g_ref[...]).astype(BF16)
    qt = lax.dot_general(wqt_ref[...], hn, _NT_DIMS, preferred_element_type=F32)
    qt_ref[...] = (qt * (ATTN_SCALE * LOG2E)).astype(BF16)
    k_ref[...] = jnp.dot(hn, wk_ref[...], preferred_element_type=F32).astype(BF16)
    vt_ref[...] = lax.dot_general(wvt_ref[...], hn, _NT_DIMS, preferred_element_type=F32).astype(BF16)
    zf_ref[...] = jnp.dot(hn, wf_ref[...], preferred_element_type=F32)


def _qkv_proj(h, g, w_in):
    bsz, seq, d = h.shape
    tile = min(TOKEN_TILE, seq)
    assert seq % tile == 0
    w_qt = w_in[:, :ATT_W].T.astype(BF16)
    w_k = w_in[:, ATT_W:2 * ATT_W].astype(BF16)
    w_vt = w_in[:, 2 * ATT_W:3 * ATT_W].T.astype(BF16)
    w_f = jnp.pad(w_in[:, 3 * ATT_W:], ((0, 0), (0, LANES - H_FOX))).astype(BF16)
    act_spec = pl.BlockSpec((None, tile, d), lambda b, s: (b, s, 0))
    tok_major = pl.BlockSpec((None, tile, ATT_W), lambda b, s: (b, s, 0))
    feat_major = pl.BlockSpec((None, ATT_W, tile), lambda b, s: (b, 0, s))
    zf_spec = pl.BlockSpec((None, tile, LANES), lambda b, s: (b, s, 0))
    return pl.pallas_call(
        _qkv_kernel,
        grid=(bsz, seq // tile),
        in_specs=[act_spec, _const_spec((1, d)), _const_spec(w_qt.shape), _const_spec(w_k.shape),
                  _const_spec(w_vt.shape), _const_spec(w_f.shape)],
        out_specs=[feat_major, tok_major, feat_major, zf_spec],
        out_shape=[jax.ShapeDtypeStruct((bsz, ATT_W, seq), BF16),
                   jax.ShapeDtypeStruct((bsz, seq, ATT_W), BF16),
                   jax.ShapeDtypeStruct((bsz, ATT_W, seq), BF16),
                   jax.ShapeDtypeStruct((bsz, seq, LANES), F32)],
        compiler_params=_params(("arbitrary", "arbitrary")),
        name="qkv_proj",
    )(h, g.reshape(1, d), w_qt, w_k, w_vt, w_f)


def _fgate_kernel(zf_ref, bf_ref, o_ref, *, seq):
    z = zf_ref[...] + bf_ref[...]
    x = -(jnp.maximum(-z, 0.0) + jnp.log1p(jnp.exp(-jnp.abs(z))))
    row = lax.broadcasted_iota(jnp.int32, (seq, LANES), 0)
    sh = 1
    while sh < seq:
        x = x + jnp.where(row >= sh, pltpu.roll(x, sh, 0), 0.0)
        sh *= 2
    x = x * LOG2E
    pieces = []
    for _ in range(F_PIECES):
        piece = x.astype(BF16)
        pieces.append(piece)
        x = x - piece.astype(F32)
    src = lax.broadcasted_iota(jnp.int32, (F_PIECES * LANES, N_PAIR * LANES), 0)
    dst = lax.broadcasted_iota(jnp.int32, (F_PIECES * LANES, N_PAIR * LANES), 1)
    piece_id, head = src // LANES, src % LANES
    pair, slot = dst // LANES, dst % LANES
    hit = (head // 2 == pair) & (slot == F_PIECES * (head % 2) + piece_id) & (head < H_FOX)
    place = jnp.where(hit, 1.0, 0.0).astype(BF16)
    o_ref[...] = jnp.dot(jnp.concatenate(pieces, axis=-1), place,
                         preferred_element_type=F32).astype(BF16)


def _fgate_cumsum(zf, b_f):
    bsz, seq, _ = zf.shape
    return pl.pallas_call(
        functools.partial(_fgate_kernel, seq=seq),
        grid=(bsz,),
        in_specs=[pl.BlockSpec((None, seq, LANES), lambda b: (b, 0, 0)), _const_spec((1, LANES))],
        out_specs=pl.BlockSpec((None, seq, N_PAIR * LANES), lambda b: (b, 0, 0)),
        out_shape=jax.ShapeDtypeStruct((bsz, seq, N_PAIR * LANES), BF16),
        compiler_params=_params(("arbitrary",)),
        name="fgate_cumsum",
    )(zf, jnp.pad(b_f, (0, LANES - H_FOX)).reshape(1, LANES))


ACC_ROWS = HEAD_DIM + BF16_ROWS


def _softmax_init(m_sc, acc_sc):
    m_sc[...] = jnp.full_like(m_sc, MASK_VALUE)
    acc_sc[...] = jnp.zeros_like(acc_sc)


def _fill_value_rows(vaug_sc, vt_ref, n_heads):
    ones = jnp.ones((BF16_ROWS, vt_ref.shape[1]), BF16)
    for h in range(n_heads):
        vaug_sc[h, 0:HEAD_DIM, :] = vt_ref[h * HEAD_DIM:(h + 1) * HEAD_DIM, :]
        vaug_sc[h, HEAD_DIM:, :] = ones


def _flash_pipeline(n_tiles, first, second, rest, key_start, tk, n_heads,
                    vaug_sc, s_sc, p_sc, a_sc, m_sc, acc_sc):
    def stage_q(tiles, slot):
        for h in range(n_heads):
            s_sc[slot, h] = tiles[h]

    def stage_x(slot):
        for h in range(n_heads):
            m_prev = m_sc[h]
            m_new = jnp.maximum(m_prev, jnp.max(s_sc[slot, h], axis=0, keepdims=True))
            a_sc[slot, h] = jnp.exp2(m_prev - m_new)
            p_sc[slot, h] = jnp.exp2(s_sc[slot, h] - m_new).astype(BF16)
            m_sc[h] = m_new

    def stage_v(u, slot):
        k0 = key_start(u)
        for h in range(n_heads):
            pv = jnp.dot(vaug_sc[h, :, pl.ds(k0, tk)], p_sc[slot, h], preferred_element_type=F32)
            acc_sc[h] = a_sc[slot, h] * acc_sc[h] + pv

    def step(u, slot):
        stage_q(rest(u + 1), 1 - slot)
        stage_v(u - 1, 1 - slot)
        stage_x(slot)

    n_eff = jnp.maximum(n_tiles, 2)
    _softmax_init(m_sc, acc_sc)
    stage_q(first(), 0)
    stage_q(second(), 1)
    stage_x(0)

    n_steps = n_eff - 2

    def pair_body(k, carry):
        u = 2 * k + 1
        step(u, 1)
        step(u + 1, 0)
        return carry

    lax.fori_loop(0, n_steps // 2, pair_body, 0)

    @pl.when(n_steps % 2 == 1)
    def _():
        step(n_eff - 2, 1)

    for parity in range(2):
        @pl.when(n_eff % 2 == parity)
        def _():
            stage_v(n_eff - 2, parity)
            stage_x(1 - parity)
            stage_v(n_eff - 1, 1 - parity)


def _softmax_finish(o_ref, acc_sc, n_heads):
    for pair in range(n_heads // 2):
        halves = []
        for h in (2 * pair, 2 * pair + 1):
            acc = acc_sc[h]
            halves.append(acc[0:HEAD_DIM, :] / acc[HEAD_DIM:HEAD_DIM + 1, :])
        out_t = jnp.concatenate(halves, axis=0)
        o_ref[:, pair * HEAD_PAIR_W:(pair + 1) * HEAD_PAIR_W] = jnp.transpose(out_t).astype(o_ref.dtype)


def _head_rows(qt_ref, h):
    pair, e = divmod(h, 2)
    q = qt_ref[pair * HEAD_PAIR_W:(pair + 1) * HEAD_PAIR_W, :]
    row = lax.broadcasted_iota(jnp.int32, q.shape, 0)
    mine = (row >= e * HEAD_DIM) & (row < (e + 1) * HEAD_DIM)
    return jnp.where(mine, q, jnp.zeros_like(q))


def _key_le_query(tile):
    key = lax.broadcasted_iota(jnp.int32, (tile, tile), 0)
    query = lax.broadcasted_iota(jnp.int32, (tile, tile), 1)
    return key <= query


def _fox_kernel(qt_ref, k_ref, f_ref, vt_ref, o_ref,
                kaug_sc, vaug_sc, s_sc, p_sc, a_sc, m_sc, acc_sc, *, tile):
    i = pl.program_id(1)

    @pl.when(i == 0)
    def _():
        for pair in range(N_PAIR):
            kaug_sc[pair, :, 0:HEAD_PAIR_W] = k_ref[:, pair * HEAD_PAIR_W:(pair + 1) * HEAD_PAIR_W]
            kaug_sc[pair, :, HEAD_PAIR_W:] = f_ref[:, pair * LANES:(pair + 1) * LANES]
        _fill_value_rows(vaug_sc, vt_ref, H_FOX)

    row = lax.broadcasted_iota(jnp.int32, (LANES, tile), 0)
    q_aug = []
    for h in range(H_FOX):
        e = h % 2
        minus_one = (row >= F_PIECES * e) & (row < F_PIECES * (e + 1))
        sel = jnp.where(minus_one, -1.0, 0.0).astype(BF16)
        q_aug.append(jnp.concatenate([_head_rows(qt_ref, h), sel], axis=0))

    def key_start(u):
        j = jnp.clip(jnp.where(u == 0, i, u - 1), 0, i)
        return pl.multiple_of(j * tile, tile)

    def scores(u):
        k0 = key_start(u)
        return [jnp.dot(kaug_sc[h // 2, pl.ds(k0, tile), :], q_aug[h], preferred_element_type=F32)
                for h in range(H_FOX)]

    def first():
        allowed = _key_le_query(tile)
        return [jnp.where(allowed, s_t, MASK_VALUE) for s_t in scores(0)]

    def second():
        absent = jnp.where(i >= 1, 0.0, MASK_VALUE)
        return [s_t + absent for s_t in scores(1)]

    _flash_pipeline(i + 1, first, second, scores, key_start, tile, H_FOX,
                    vaug_sc, s_sc, p_sc, a_sc, m_sc, acc_sc)
    _softmax_finish(o_ref, acc_sc, H_FOX)


def _fox_attention(qt, k, vt, faug):
    bsz, seq, _ = k.shape
    tile = min(ATT_TILE, seq)
    assert seq % tile == 0 and tile % LANES == 0
    width = H_FOX * HEAD_DIM
    return pl.pallas_call(
        functools.partial(_fox_kernel, tile=tile),
        grid=(bsz, seq // tile),
        in_specs=[pl.BlockSpec((None, width, tile), lambda b, i: (b, 0, i)),
                  pl.BlockSpec((None, seq, width), lambda b, i: (b, 0, 0)),
                  pl.BlockSpec((None, seq, N_PAIR * LANES), lambda b, i: (b, 0, 0)),
                  pl.BlockSpec((None, width, seq), lambda b, i: (b, 0, 0))],
        out_specs=pl.BlockSpec((None, tile, width), lambda b, i: (b, i, 0)),
        out_shape=jax.ShapeDtypeStruct((bsz, seq, width), BF16),
        scratch_shapes=[pltpu.VMEM((N_PAIR, seq, 2 * HEAD_PAIR_W), BF16),
                        pltpu.VMEM((H_FOX, ACC_ROWS, seq), BF16),
                        pltpu.VMEM((2, H_FOX, tile, tile), F32),
                        pltpu.VMEM((2, H_FOX, tile, tile), BF16),
                        pltpu.VMEM((2, H_FOX, 1, tile), F32),
                        pltpu.VMEM((H_FOX, 1, tile), F32),
                        pltpu.VMEM((H_FOX, ACC_ROWS, tile), F32)],
        compiler_params=_params(("arbitrary", "arbitrary")),
        name="fox_attention",
    )(qt, k, faug, vt)


def _t5_bias_tile(dist, rel_ref, head):
    exact = T5_BUCKETS // 2
    d_f = jnp.maximum(dist, 1).astype(F32)
    log_b = exact + (jnp.log(d_f / exact) / math.log(T5_MAX_DIST / exact)
                     * (T5_BUCKETS - exact)).astype(jnp.int32)
    log_b = jnp.minimum(log_b, T5_BUCKETS - 1)
    bucket = jnp.where(dist < exact, dist, log_b)
    out = jnp.zeros(dist.shape, F32)
    for b in range(T5_BUCKETS):
        out = jnp.where(bucket == b, rel_ref[b, head], out)
    return out


def _moba_kernel(rel_ref, qt_ref, k_ref, vt_ref, o_ref,
                 kmean_sc, vaug_sc, bias_sc, pen_sc, s_sc, p_sc, a_sc, m_sc, acc_sc,
                 *, seq, blk_rows):
    blk = MOBA_BLOCK
    b = pl.program_id(0)
    i = pl.program_id(1)

    @pl.when((b == 0) & (i == 0))
    def _():
        key = lax.broadcasted_iota(jnp.int32, (blk, blk), 0)
        query = lax.broadcasted_iota(jnp.int32, (blk, blk), 1)
        for h in range(H_MOBA):
            for delta in range(2):
                dist = jnp.maximum(delta * blk + query - key, 0)
                bias_sc[h, delta] = _t5_bias_tile(dist, rel_ref, h) * LOG2E

    @pl.when(i == 0)
    def _():
        mrow = lax.broadcasted_iota(jnp.int32, (blk_rows, seq), 0)
        mcol = lax.broadcasted_iota(jnp.int32, (blk_rows, seq), 1) // blk
        avg = jnp.where(mrow == mcol, 1.0 / blk, 0.0).astype(BF16)
        kmean_sc[...] = jnp.dot(avg, k_ref[...], preferred_element_type=F32)
        _fill_value_rows(vaug_sc, vt_ref, H_MOBA)

    qh = [_head_rows(qt_ref, h) for h in range(H_MOBA)]

    blk_id = lax.broadcasted_iota(jnp.int32, (blk_rows, blk), 0)
    for h in range(H_MOBA):
        pair = h // 2
        kmean = kmean_sc[:, pair * HEAD_PAIR_W:(pair + 1) * HEAD_PAIR_W]
        km_hi = kmean.astype(BF16)
        km_lo = (kmean - km_hi.astype(F32)).astype(BF16)
        gate = (jnp.dot(km_hi, qh[h], preferred_element_type=F32)
                + jnp.dot(km_lo, qh[h], preferred_element_type=F32))
        gate = jnp.where(blk_id < i, gate, -jnp.inf)
        sel = jnp.zeros(gate.shape, jnp.bool_)
        for _ in range(MOBA_TOPK):
            best = jnp.max(gate, axis=0, keepdims=True)
            first_id = jnp.min(jnp.where(gate == best, blk_id, blk_rows), axis=0, keepdims=True)
            pick = (blk_id == first_id) & (best > -jnp.inf)
            sel = sel | pick
            gate = jnp.where(pick, -jnp.inf, gate)
        pen = jnp.where(sel, 0.0, MASK_VALUE)
        pen_sc[h, 0] = pen
        pen_sc[h, 1] = pen + rel_ref[T5_BUCKETS - 1, h] * LOG2E

    def key_block(u):
        return jnp.clip(jnp.where(u == 0, i, jnp.where(u == 1, i - 1, u - 2)), 0, i)

    def key_start(u):
        return pl.multiple_of(key_block(u) * blk, blk)

    def scores(h, u):
        pair = h // 2
        k = k_ref[pl.ds(key_start(u), blk), pair * HEAD_PAIR_W:(pair + 1) * HEAD_PAIR_W]
        return jnp.dot(k, qh[h], preferred_element_type=F32)

    def first():
        allowed = _key_le_query(blk)
        return [jnp.where(allowed, scores(h, 0) + bias_sc[h, 0], MASK_VALUE) for h in range(H_MOBA)]

    def second():
        j = key_block(1)
        return [scores(h, 1) + bias_sc[h, 1] + pen_sc[h, 0, pl.ds(j, 1), :] for h in range(H_MOBA)]

    def rest(u):
        j = key_block(u)
        return [scores(h, u) + pen_sc[h, 1, pl.ds(j, 1), :] for h in range(H_MOBA)]

    _flash_pipeline(i + 1, first, second, rest, key_start, blk, H_MOBA,
                    vaug_sc, s_sc, p_sc, a_sc, m_sc, acc_sc)
    _softmax_finish(o_ref, acc_sc, H_MOBA)


def _moba_attention(qt, k, vt, rel_bias):
    bsz, seq, _ = k.shape
    blk = MOBA_BLOCK
    assert seq % blk == 0
    assert blk + 1 >= T5_MAX_DIST
    blk_rows = -(-(seq // blk) // BF16_ROWS) * BF16_ROWS
    width = H_MOBA * HEAD_DIM
    off = (H_FOX * HEAD_DIM) // width
    return pl.pallas_call(
        functools.partial(_moba_kernel, seq=seq, blk_rows=blk_rows),
        grid_spec=pltpu.PrefetchScalarGridSpec(
            num_scalar_prefetch=1,
            grid=(bsz, seq // blk),
            in_specs=[pl.BlockSpec((None, width, blk), lambda b, i, rel: (b, off, i)),
                      pl.BlockSpec((None, seq, width), lambda b, i, rel: (b, 0, off)),
                      pl.BlockSpec((None, width, seq), lambda b, i, rel: (b, off, 0))],
            out_specs=pl.BlockSpec((None, blk, width), lambda b, i, rel: (b, i, 0)),
            scratch_shapes=[pltpu.VMEM((blk_rows, width), F32),
                            pltpu.VMEM((H_MOBA, ACC_ROWS, seq), BF16),
                            pltpu.VMEM((H_MOBA, 2, blk, blk), F32),
                            pltpu.VMEM((H_MOBA, 2, blk_rows, blk), F32),
                            pltpu.VMEM((2, H_MOBA, blk, blk), F32),
                            pltpu.VMEM((2, H_MOBA, blk, blk), BF16),
                            pltpu.VMEM((2, H_MOBA, 1, blk), F32),
                            pltpu.VMEM((H_MOBA, 1, blk), F32),
                            pltpu.VMEM((H_MOBA, ACC_ROWS, blk), F32)]),
        out_shape=jax.ShapeDtypeStruct((bsz, seq, width), BF16),
        compiler_params=_params(("arbitrary", "arbitrary")),
        name="moba_attention",
    )(rel_bias, qt, k, vt)


def _attn_out_kernel(h_ref, yc_ref, yd_ref, w_ref, o_ref):
    wc = H_FOX * HEAD_DIM
    o_ref[...] = (h_ref[...]
                  + jnp.dot(yc_ref[...], w_ref[0:wc, :], preferred_element_type=F32)
                  + jnp.dot(yd_ref[...], w_ref[wc:, :], preferred_element_type=F32))


def _attn_out(h, y_c, y_d, w_out):
    bsz, seq, d = h.shape
    tile = min(TOKEN_TILE, seq)
    act_spec = pl.BlockSpec((None, tile, d), lambda b, s: (b, s, 0))
    yc_spec = pl.BlockSpec((None, tile, y_c.shape[-1]), lambda b, s: (b, s, 0))
    yd_spec = pl.BlockSpec((None, tile, y_d.shape[-1]), lambda b, s: (b, s, 0))
    return pl.pallas_call(
        _attn_out_kernel,
        grid=(bsz, seq // tile),
        in_specs=[act_spec, yc_spec, yd_spec, _const_spec(w_out.shape)],
        out_specs=act_spec,
        out_shape=jax.ShapeDtypeStruct(h.shape, F32),
        compiler_params=_params(("arbitrary", "arbitrary")),
        name="attn_out",
    )(h, y_c, y_d, w_out.astype(BF16))


def kernel(x, mix_norm_g, ffn_norm_g, final_norm_g, ev_w_in, ev_conv_w, ev_pool_w, ev_pool_scale,
           ev_w_out, od_w_in, od_b_f, od_w_out, rel_bias, ffn_w_in, ffn_conv_w, ffn_conv_b, ffn_w_out):
    depth = mix_norm_g.shape[0]
    h = x
    for layer in range(depth):
        if layer % 2 == 0:
            e = layer // 2
            h = _even_layer(h, mix_norm_g[layer], ev_w_in[e], ev_conv_w[e], ev_pool_w[e],
                            ev_pool_scale[e], ev_w_out[e])
        else:
            o = layer // 2
            qt, k, vt, zf = _qkv_proj(h, mix_norm_g[layer], od_w_in[o])
            faug = _fgate_cumsum(zf, od_b_f[o])
            y_c = _fox_attention(qt, k, vt, faug)
            y_d = _moba_attention(qt, k, vt, rel_bias)
            h = _attn_out(h, y_c, y_d, od_w_out[o])
        h = _ffn_layer(h, ffn_norm_g[layer], ffn_w_in[layer], ffn_conv_w[layer], ffn_conv_b[layer],
                       ffn_w_out[layer], final_norm_g, final_norm=(layer == depth - 1))
    return h
```

```python
import functools
import math

import jax
import jax.numpy as jnp
from jax import lax
from jax.experimental import pallas as pl
from jax.experimental.pallas import tpu as pltpu

F32 = jnp.float32
BF16 = jnp.bfloat16

D_MODEL = 1024
HEAD_DIM = 64
CONV_W = 3
A_W = D_MODEL // 2
B_W = D_MODEL - A_W
POOL_WINDOWS = (2, 4, 8, 16)
POOL_G = B_W // len(POOL_WINDOWS)
H_FOX = D_MODEL // (2 * HEAD_DIM)
H_MOBA = D_MODEL // (2 * HEAD_DIM)
ATT_W = (H_FOX + H_MOBA) * HEAD_DIM
ATTN_SCALE = HEAD_DIM ** -0.5
MOBA_BLOCK = 256
MOBA_TOPK = 3
T5_BUCKETS = 32
T5_MAX_DIST = 128
D_FF = 2816
RMS_EPS = 1e-6

LANES = 128
SUBLANES = 8
BF16_ROWS = 16
VMEM_LIMIT_BYTES = 56 * 1024 * 1024

HEAD_PAIR_W = 2 * HEAD_DIM
N_PAIR = H_FOX // 2
MASK_VALUE = -1e30
LOG2E = math.log2(math.e)
F_PIECES = 3

TOKEN_TILE = 512
FFN_CHUNK = D_FF
ATT_TILE = MOBA_BLOCK
ATT_SUBTILES = 2
POOL_HALO = 32


def _rmsnorm(x, g):
    ms = jnp.mean(x * x, axis=-1, keepdims=True)
    return x * lax.rsqrt(ms + RMS_EPS) * g


def _const_spec(shape):
    nd = len(shape)
    return pl.BlockSpec(shape, lambda *_: (0,) * nd, pipeline_mode=pl.Buffered(1))


def _params(semantics):
    return pltpu.CompilerParams(dimension_semantics=semantics, vmem_limit_bytes=VMEM_LIMIT_BYTES)


def _even_kernel(h_ref, g_ref, win_ref, cw_ref, pw_ref, ps_ref, wout_ref, o_ref,
                 cv_sc, u_sc, lvl_sc, tailcv_sc, tailu_sc, *, tile):
    s = pl.program_id(1)
    halo = POOL_HALO

    @pl.when(s == 0)
    def _():
        tailcv_sc[...] = jnp.zeros_like(tailcv_sc)
        tailu_sc[...] = jnp.zeros_like(tailu_sc)

    h = h_ref[...]
    hn = _rmsnorm(h, g_ref[...]).astype(BF16)
    z = jnp.dot(hn, win_ref[...], preferred_element_type=F32)
    gate_b = z[:, :A_W]
    cv = z[:, A_W:2 * A_W] * z[:, 2 * A_W:3 * A_W]
    pool_in = z[:, 3 * A_W:]

    cv_sc[0:SUBLANES, :] = tailcv_sc[...]
    cv_sc[SUBLANES:SUBLANES + tile, :] = cv
    tailcv_sc[...] = cv[tile - SUBLANES:, :]
    conv = (cw_ref[0:1, :] * cv_sc[SUBLANES - 2:SUBLANES - 2 + tile, :]
            + cw_ref[1:2, :] * cv_sc[SUBLANES - 1:SUBLANES - 1 + tile, :]
            + cw_ref[2:3, :] * cv)
    y_a = gate_b * conv

    u_sc[0:halo, :] = tailu_sc[...]
    u_sc[halo:halo + tile, :] = pool_in
    tailu_sc[...] = pool_in[tile - halo:, :]
    pos = s * tile + lax.broadcasted_iota(jnp.int32, (tile, 1), 0)
    ys = [y_a.astype(BF16)]
    for g, w in enumerate(POOL_WINDOWS):
        cols = slice(g * POOL_G, (g + 1) * POOL_G)
        n_lvl = g + 1
        src, src_cols = u_sc, cols
        for lvl in range(1, n_lvl):
            sh = 2 ** (lvl - 1)
            lo = SUBLANES * lvl
            dst = lvl_sc.at[lvl % 2]
            dst[lo:halo + tile, :] = (src[lo:halo + tile, src_cols]
                                      + src[lo - sh:halo + tile - sh, src_cols])
            src, src_cols = dst, slice(None)
        sh = 2 ** (n_lvl - 1)
        wsum = src[halo:halo + tile, src_cols] + src[halo - sh:halo + tile - sh, src_cols]
        cnt = jnp.minimum(pos + 1, w).astype(F32)
        p = wsum / cnt - pool_in[:, cols]
        yb = jnp.dot(p.astype(BF16), pw_ref[g], preferred_element_type=F32) * ps_ref[:, cols]
        ys.append(yb.astype(BF16))
    y = jnp.concatenate(ys, axis=-1)
    o_ref[...] = h + jnp.dot(y, wout_ref[...], preferred_element_type=F32)


def _even_layer(h, g, w_in, conv_w, pool_w, pool_scale, w_out):
    bsz, seq, d = h.shape
    tile = min(TOKEN_TILE, seq)
    assert seq % tile == 0 and tile >= POOL_HALO
    act_spec = pl.BlockSpec((None, tile, d), lambda b, s: (b, s, 0))
    return pl.pallas_call(
        functools.partial(_even_kernel, tile=tile),
        grid=(bsz, seq // tile),
        in_specs=[act_spec, _const_spec((1, d)), _const_spec(w_in.shape), _const_spec(conv_w.shape),
                  _const_spec(pool_w.shape), _const_spec((1, B_W)), _const_spec(w_out.shape)],
        out_specs=act_spec,
        out_shape=jax.ShapeDtypeStruct(h.shape, F32),
        scratch_shapes=[pltpu.VMEM((SUBLANES + tile, A_W), F32),
                        pltpu.VMEM((POOL_HALO + tile, B_W), F32),
                        pltpu.VMEM((2, POOL_HALO + tile, POOL_G), F32),
                        pltpu.VMEM((SUBLANES, A_W), F32),
                        pltpu.VMEM((POOL_HALO, B_W), F32)],
        compiler_params=_params(("arbitrary", "arbitrary")),
        name="even_mixer",
    )(h, g.reshape(1, d), w_in.astype(BF16), conv_w, pool_w.astype(BF16),
      pool_scale.reshape(1, B_W), w_out.astype(BF16))


def _ffn_kernel(h_ref, g_ref, win_ref, cw_ref, cb_ref, wout_ref, fg_ref, o_ref, ext_sc, tail_sc,
                *, tile, final_norm):
    s = pl.program_id(1)

    @pl.when(s == 0)
    def _():
        tail_sc[...] = jnp.zeros_like(tail_sc)

    h = h_ref[...]
    hn = _rmsnorm(h, g_ref[...]).astype(BF16)
    acc = h
    for c in range(D_FF // FFN_CHUNK):
        cols = slice(c * FFN_CHUNK, (c + 1) * FFN_CHUNK)
        gcols = slice(D_FF + c * FFN_CHUNK, D_FF + (c + 1) * FFN_CHUNK)
        u = jnp.dot(hn, win_ref[:, cols], preferred_element_type=F32)
        gt = jnp.dot(hn, win_ref[:, gcols], preferred_element_type=F32)
        ext_sc[0:SUBLANES, :] = tail_sc[:, cols]
        ext_sc[SUBLANES:SUBLANES + tile, :] = u
        tail_sc[:, cols] = u[tile - SUBLANES:, :]
        a = (cw_ref[0:1, cols] * ext_sc[SUBLANES - 2:SUBLANES - 2 + tile, :]
             + cw_ref[1:2, cols] * ext_sc[SUBLANES - 1:SUBLANES - 1 + tile, :]
             + cw_ref[2:3, cols] * u
             + cb_ref[:, cols])
        act = a * (1.0 / (1.0 + jnp.exp(-a))) * gt
        acc = acc + jnp.dot(act.astype(BF16), wout_ref[cols, :], preferred_element_type=F32)
    if final_norm:
        acc = _rmsnorm(acc, fg_ref[...])
    o_ref[...] = acc


def _ffn_layer(h, g, w_in, conv_w, conv_b, w_out, final_g, final_norm):
    bsz, seq, d = h.shape
    tile = min(TOKEN_TILE, seq)
    assert seq % tile == 0
    act_spec = pl.BlockSpec((None, tile, d), lambda b, s: (b, s, 0))
    return pl.pallas_call(
        functools.partial(_ffn_kernel, tile=tile, final_norm=final_norm),
        grid=(bsz, seq // tile),
        in_specs=[act_spec, _const_spec((1, d)), _const_spec(w_in.shape), _const_spec(conv_w.shape),
                  _const_spec((1, D_FF)), _const_spec(w_out.shape), _const_spec((1, d))],
        out_specs=act_spec,
        out_shape=jax.ShapeDtypeStruct(h.shape, F32),
        scratch_shapes=[pltpu.VMEM((SUBLANES + tile, FFN_CHUNK), F32),
                        pltpu.VMEM((SUBLANES, D_FF), F32)],
        compiler_params=_params(("arbitrary", "arbitrary")),
        name="conv_ffn",
    )(h, g.reshape(1, d), w_in.astype(BF16), conv_w, conv_b.reshape(1, D_FF), w_out.astype(BF16),
      final_g.reshape(1, d))


_NT_DIMS = (((1,), (1,)), ((), ()))


def _qkv_kernel(h_ref, g_ref, wqt_ref, wk_ref, wvt_ref, wf_ref, qt_ref, k_ref, vt_ref, zf_ref):
    hn = _rmsnorm(h_ref[...], g_ref[...]).astype(BF16)
    qt = lax.dot_general(wqt_ref[...], hn, _NT_DIMS, preferred_element_type=F32)
    qt_ref[...] = (qt * (ATTN_SCALE * LOG2E)).astype(BF16)
    k_ref[...] = jnp.dot(hn, wk_ref[...], preferred_element_type=F32).astype(BF16)
    vt_ref[...] = lax.dot_general(wvt_ref[...], hn, _NT_DIMS, preferred_element_type=F32).astype(BF16)
    zf_ref[...] = jnp.dot(hn, wf_ref[...], preferred_element_type=F32)


def _qkv_proj(h, g, w_in):
    bsz, seq, d = h.shape
    tile = min(TOKEN_TILE, seq)
    assert seq % tile == 0
    w_qt = w_in[:, :ATT_W].T.astype(BF16)
    w_k = w_in[:, ATT_W:2 * ATT_W].astype(BF16)
    w_vt = w_in[:, 2 * ATT_W:3 * ATT_W].T.astype(BF16)
    w_f = jnp.pad(w_in[:, 3 * ATT_W:], ((0, 0), (0, LANES - H_FOX))).astype(BF16)
    act_spec = pl.BlockSpec((None, tile, d), lambda b, s: (b, s, 0))
    tok_major = pl.BlockSpec((None, tile, ATT_W), lambda b, s: (b, s, 0))
    feat_major = pl.BlockSpec((None, ATT_W, tile), lambda b, s: (b, 0, s))
    zf_spec = pl.BlockSpec((None, tile, LANES), lambda b, s: (b, s, 0))
    return pl.pallas_call(
        _qkv_kernel,
        grid=(bsz, seq // tile),
        in_specs=[act_spec, _const_spec((1, d)), _const_spec(w_qt.shape), _const_spec(w_k.shape),
                  _const_spec(w_vt.shape), _const_spec(w_f.shape)],
        out_specs=[feat_major, tok_major, feat_major, zf_spec],
        out_shape=[jax.ShapeDtypeStruct((bsz, ATT_W, seq), BF16),
                   jax.ShapeDtypeStruct((bsz, seq, ATT_W), BF16),
                   jax.ShapeDtypeStruct((bsz, ATT_W, seq), BF16),
                   jax.ShapeDtypeStruct((bsz, seq, LANES), F32)],
        compiler_params=_params(("arbitrary", "arbitrary")),
        name="qkv_proj",
    )(h, g.reshape(1, d), w_qt, w_k, w_vt, w_f)


def _fgate_kernel(zf_ref, bf_ref, o_ref, *, seq):
    z = zf_ref[...] + bf_ref[...]
    x = -(jnp.maximum(-z, 0.0) + jnp.log1p(jnp.exp(-jnp.abs(z))))
    row = lax.broadcasted_iota(jnp.int32, (seq, LANES), 0)
    sh = 1
    while sh < seq:
        x = x + jnp.where(row >= sh, pltpu.roll(x, sh, 0), 0.0)
        sh *= 2
    x = x * LOG2E
    pieces = []
    for _ in range(F_PIECES):
        piece = x.astype(BF16)
        pieces.append(piece)
        x = x - piece.astype(F32)
    src = lax.broadcasted_iota(jnp.int32, (F_PIECES * LANES, N_PAIR * LANES), 0)
    dst = lax.broadcasted_iota(jnp.int32, (F_PIECES * LANES, N_PAIR * LANES), 1)
    piece_id, head = src // LANES, src % LANES
    pair, slot = dst // LANES, dst % LANES
    hit = (head // 2 == pair) & (slot == F_PIECES * (head % 2) + piece_id) & (head < H_FOX)
    place = jnp.where(hit, 1.0, 0.0).astype(BF16)
    o_ref[...] = jnp.dot(jnp.concatenate(pieces, axis=-1), place,
                         preferred_element_type=F32).astype(BF16)


def _fgate_cumsum(zf, b_f):
    bsz, seq, _ = zf.shape
    return pl.pallas_call(
        functools.partial(_fgate_kernel, seq=seq),
        grid=(bsz,),
        in_specs=[pl.BlockSpec((None, seq, LANES), lambda b: (b, 0, 0)), _const_spec((1, LANES))],
        out_specs=pl.BlockSpec((None, seq, N_PAIR * LANES), lambda b: (b, 0, 0)),
        out_shape=jax.ShapeDtypeStruct((bsz, seq, N_PAIR * LANES), BF16),
        compiler_params=_params(("arbitrary",)),
        name="fgate_cumsum",
    )(zf, jnp.pad(b_f, (0, LANES - H_FOX)).reshape(1, LANES))


ACC_ROWS = HEAD_DIM + BF16_ROWS


def _softmax_init(m_sc, acc_sc):
    m_sc[...] = jnp.full_like(m_sc, MASK_VALUE)
    acc_sc[...] = jnp.zeros_like(acc_sc)


def _fill_value_rows(vaug_sc, vt_ref, n_heads):
    ones = jnp.ones((BF16_ROWS, vt_ref.shape[1]), BF16)
    for h in range(n_heads):
        vaug_sc[h, 0:HEAD_DIM, :] = vt_ref[h * HEAD_DIM:(h + 1) * HEAD_DIM, :]
        vaug_sc[h, HEAD_DIM:, :] = ones


def _flash_pipeline(n_tiles, first, second, rest, key_start, tk, n_heads,
                    vaug_sc, slots, m_sc, acc_sc):
    def stage_q(tiles, slot):
        s_sc, t_sc, _, _ = slots[slot]
        for h in range(n_heads):
            s_sc[h] = tiles[h]
            t_sc[h] = jnp.max(tiles[h], axis=0, keepdims=True)

    def stage_x(slot):
        s_sc, t_sc, p_sc, a_sc = slots[slot]
        for h in range(n_heads):
            m_prev = m_sc[h]
            m_new = jnp.maximum(m_prev, t_sc[h])
            a_sc[h] = jnp.exp2(m_prev - m_new)
            p_sc[h] = jnp.exp2(s_sc[h] - m_new).astype(BF16)
            m_sc[h] = m_new

    def stage_v(u, slot):
        _, _, p_sc, a_sc = slots[slot]
        k0 = key_start(u)
        for h in range(n_heads):
            pv = jnp.dot(vaug_sc[h, :, pl.ds(k0, tk)], p_sc[h], preferred_element_type=F32)
            acc_sc[h] = a_sc[h] * acc_sc[h] + pv

    def step(u, slot):
        stage_q(rest(u + 1), 1 - slot)
        stage_v(u - 1, 1 - slot)
        stage_x(slot)

    n_eff = jnp.maximum(n_tiles, 2)
    _softmax_init(m_sc, acc_sc)
    stage_q(first(), 0)
    stage_q(second(), 1)
    stage_x(0)

    n_steps = n_eff - 2

    def pair_body(k, carry):
        u = 2 * k + 1
        step(u, 1)
        step(u + 1, 0)
        return carry

    lax.fori_loop(0, n_steps // 2, pair_body, 0)

    @pl.when(n_steps % 2 == 1)
    def _():
        step(n_eff - 2, 1)

    for parity in range(2):
        @pl.when(n_eff % 2 == parity)
        def _():
            stage_v(n_eff - 2, parity)
            stage_x(1 - parity)
            stage_v(n_eff - 1, 1 - parity)


def _pipeline_scratch(n_heads, tile):
    one = [pltpu.VMEM((n_heads, tile, tile), F32), pltpu.VMEM((n_heads, 1, tile), F32),
           pltpu.VMEM((n_heads, tile, tile), BF16), pltpu.VMEM((n_heads, 1, tile), F32)]
    return one + one


def _softmax_finish(acc_sc, n_heads):
    blocks = []
    for pair in range(n_heads // 2):
        halves = []
        for h in (2 * pair, 2 * pair + 1):
            acc = acc_sc[h]
            halves.append(acc[0:HEAD_DIM, :] / acc[HEAD_DIM:HEAD_DIM + 1, :])
        out_t = jnp.concatenate(halves, axis=0)
        blocks.append(jnp.transpose(out_t).astype(BF16))
    return blocks


def _head_rows(qt_ref, h, q0, tq):
    pair, e = divmod(h, 2)
    q = qt_ref[pair * HEAD_PAIR_W:(pair + 1) * HEAD_PAIR_W, pl.ds(q0, tq)]
    row = lax.broadcasted_iota(jnp.int32, q.shape, 0)
    mine = (row >= e * HEAD_DIM) & (row < (e + 1) * HEAD_DIM)
    return jnp.where(mine, q, jnp.zeros_like(q))


def _key_le_query(tile):
    key = lax.broadcasted_iota(jnp.int32, (tile, tile), 0)
    query = lax.broadcasted_iota(jnp.int32, (tile, tile), 1)
    return key <= query


def _fox_kernel(qt_ref, k_ref, f_ref, vt_ref, o_ref,
                kaug_sc, vaug_sc, m_sc, acc_sc, *slot_refs, tile, n_sub):
    g = pl.program_id(1)

    @pl.when(g == 0)
    def _():
        for pair in range(N_PAIR):
            kaug_sc[pair, :, 0:HEAD_PAIR_W] = k_ref[:, pair * HEAD_PAIR_W:(pair + 1) * HEAD_PAIR_W]
            kaug_sc[pair, :, HEAD_PAIR_W:] = f_ref[:, pair * LANES:(pair + 1) * LANES]
        _fill_value_rows(vaug_sc, vt_ref, H_FOX)

    def query_tile(sub, carry):
        i = g * n_sub + sub
        q0 = pl.multiple_of(sub * tile, tile)
        row = lax.broadcasted_iota(jnp.int32, (LANES, tile), 0)
        q_aug = []
        for h in range(H_FOX):
            e = h % 2
            minus_one = (row >= F_PIECES * e) & (row < F_PIECES * (e + 1))
            sel = jnp.where(minus_one, -1.0, 0.0).astype(BF16)
            q_aug.append(jnp.concatenate([_head_rows(qt_ref, h, q0, tile), sel], axis=0))

        def key_start(u):
            j = jnp.clip(jnp.where(u == 0, i, u - 1), 0, i)
            return pl.multiple_of(j * tile, tile)

        def scores(u):
            k0 = key_start(u)
            return [jnp.dot(kaug_sc[h // 2, pl.ds(k0, tile), :], q_aug[h],
                            preferred_element_type=F32) for h in range(H_FOX)]

        def first():
            allowed = _key_le_query(tile)
            return [jnp.where(allowed, s_t, MASK_VALUE) for s_t in scores(0)]

        def second():
            absent = jnp.where(i >= 1, 0.0, MASK_VALUE)
            return [s_t + absent for s_t in scores(1)]

        _flash_pipeline(i + 1, first, second, scores, key_start, tile, H_FOX,
                        vaug_sc, (slot_refs[:4], slot_refs[4:]), m_sc, acc_sc)
        for pair, block in enumerate(_softmax_finish(acc_sc, H_FOX)):
            o_ref[pl.ds(q0, tile), pair * HEAD_PAIR_W:(pair + 1) * HEAD_PAIR_W] = block
        return carry

    lax.fori_loop(0, n_sub, query_tile, 0)


def _fox_attention(qt, k, vt, faug):
    bsz, seq, _ = k.shape
    tile = min(ATT_TILE, seq)
    n_sub = min(ATT_SUBTILES, seq // tile)
    step = tile * n_sub
    assert seq % step == 0 and tile % LANES == 0
    width = H_FOX * HEAD_DIM
    return pl.pallas_call(
        functools.partial(_fox_kernel, tile=tile, n_sub=n_sub),
        grid=(bsz, seq // step),
        in_specs=[pl.BlockSpec((None, width, step), lambda b, g: (b, 0, g)),
                  pl.BlockSpec((None, seq, width), lambda b, g: (b, 0, 0)),
                  pl.BlockSpec((None, seq, N_PAIR * LANES), lambda b, g: (b, 0, 0)),
                  pl.BlockSpec((None, width, seq), lambda b, g: (b, 0, 0))],
        out_specs=pl.BlockSpec((None, step, width), lambda b, g: (b, g, 0)),
        out_shape=jax.ShapeDtypeStruct((bsz, seq, width), BF16),
        scratch_shapes=[pltpu.VMEM((N_PAIR, seq, 2 * HEAD_PAIR_W), BF16),
                        pltpu.VMEM((H_FOX, ACC_ROWS, seq), BF16),
                        pltpu.VMEM((H_FOX, 1, tile), F32),
                        pltpu.VMEM((H_FOX, ACC_ROWS, tile), F32)] + _pipeline_scratch(H_FOX, tile),
        compiler_params=_params(("arbitrary", "arbitrary")),
        name="fox_attention",
    )(qt, k, faug, vt)


def _t5_bias_tile(dist, rel_ref, head):
    exact = T5_BUCKETS // 2
    d_f = jnp.maximum(dist, 1).astype(F32)
    log_b = exact + (jnp.log(d_f / exact) / math.log(T5_MAX_DIST / exact)
                     * (T5_BUCKETS - exact)).astype(jnp.int32)
    log_b = jnp.minimum(log_b, T5_BUCKETS - 1)
    bucket = jnp.where(dist < exact, dist, log_b)
    out = jnp.zeros(dist.shape, F32)
    for b in range(T5_BUCKETS):
        out = jnp.where(bucket == b, rel_ref[b, head], out)
    return out


def _moba_kernel(rel_ref, qt_ref, k_ref, vt_ref, h_ref, yc_ref, wo_ref, o_ref,
                 kmean_sc, vaug_sc, bias_sc, pen_sc, m_sc, acc_sc, *slot_refs,
                 seq, blk_rows, n_sub):
    blk = MOBA_BLOCK
    b = pl.program_id(0)
    g = pl.program_id(1)

    @pl.when((b == 0) & (g == 0))
    def _():
        key = lax.broadcasted_iota(jnp.int32, (blk, blk), 0)
        query = lax.broadcasted_iota(jnp.int32, (blk, blk), 1)
        for h in range(H_MOBA):
            for delta in range(2):
                dist = jnp.maximum(delta * blk + query - key, 0)
                bias_sc[h, delta] = _t5_bias_tile(dist, rel_ref, h) * LOG2E

    @pl.when(g == 0)
    def _():
        mrow = lax.broadcasted_iota(jnp.int32, (blk_rows, seq), 0)
        mcol = lax.broadcasted_iota(jnp.int32, (blk_rows, seq), 1) // blk
        avg = jnp.where(mrow == mcol, 1.0 / blk, 0.0).astype(BF16)
        kmean_sc[...] = jnp.dot(avg, k_ref[...], preferred_element_type=F32)
        _fill_value_rows(vaug_sc, vt_ref, H_MOBA)

    def query_tile(sub, carry):
        i = g * n_sub + sub
        q0 = pl.multiple_of(sub * blk, blk)
        qh = [_head_rows(qt_ref, h, q0, blk) for h in range(H_MOBA)]

        blk_id = lax.broadcasted_iota(jnp.int32, (blk_rows, blk), 0)
        for h in range(H_MOBA):
            pair = h // 2
            kmean = kmean_sc[:, pair * HEAD_PAIR_W:(pair + 1) * HEAD_PAIR_W]
            km_hi = kmean.astype(BF16)
            km_lo = (kmean - km_hi.astype(F32)).astype(BF16)
            gate = (jnp.dot(km_hi, qh[h], preferred_element_type=F32)
                    + jnp.dot(km_lo, qh[h], preferred_element_type=F32))
            gate = jnp.where(blk_id < i, gate, -jnp.inf)
            sel = jnp.zeros(gate.shape, jnp.bool_)
            for _ in range(MOBA_TOPK):
                best = jnp.max(gate, axis=0, keepdims=True)
                first_id = jnp.min(jnp.where(gate == best, blk_id, blk_rows), axis=0, keepdims=True)
                pick = (blk_id == first_id) & (best > -jnp.inf)
                sel = sel | pick
                gate = jnp.where(pick, -jnp.inf, gate)
            pen = jnp.where(sel, 0.0, MASK_VALUE)
            pen_sc[h, 0] = pen
            pen_sc[h, 1] = pen + rel_ref[T5_BUCKETS - 1, h] * LOG2E

        def key_block(u):
            return jnp.clip(jnp.where(u == 0, i, jnp.where(u == 1, i - 1, u - 2)), 0, i)

        def key_start(u):
            return pl.multiple_of(key_block(u) * blk, blk)

        def scores(h, u):
            pair = h // 2
            k = k_ref[pl.ds(key_start(u), blk), pair * HEAD_PAIR_W:(pair + 1) * HEAD_PAIR_W]
            return jnp.dot(k, qh[h], preferred_element_type=F32)

        def first():
            allowed = _key_le_query(blk)
            return [jnp.where(allowed, scores(h, 0) + bias_sc[h, 0], MASK_VALUE)
                    for h in range(H_MOBA)]

        def second():
            j = key_block(1)
            return [scores(h, 1) + bias_sc[h, 1] + pen_sc[h, 0, pl.ds(j, 1), :]
                    for h in range(H_MOBA)]

        def rest(u):
            j = key_block(u)
            return [scores(h, u) + pen_sc[h, 1, pl.ds(j, 1), :] for h in range(H_MOBA)]

        _flash_pipeline(i + 1, first, second, rest, key_start, blk, H_MOBA,
                        vaug_sc, (slot_refs[:4], slot_refs[4:]), m_sc, acc_sc)

        y_d = jnp.concatenate(_softmax_finish(acc_sc, H_MOBA), axis=-1)
        wc = yc_ref.shape[-1]
        rows = pl.ds(q0, blk)
        o_ref[rows, :] = (h_ref[rows, :]
                          + jnp.dot(yc_ref[rows, :], wo_ref[0:wc, :], preferred_element_type=F32)
                          + jnp.dot(y_d, wo_ref[wc:, :], preferred_element_type=F32))
        return carry

    lax.fori_loop(0, n_sub, query_tile, 0)


def _moba_attention_out(qt, k, vt, rel_bias, h, y_c, w_out):
    bsz, seq, d = h.shape
    blk = MOBA_BLOCK
    n_sub = min(ATT_SUBTILES, seq // blk)
    step = blk * n_sub
    assert seq % step == 0
    assert blk + 1 >= T5_MAX_DIST
    blk_rows = -(-(seq // blk) // BF16_ROWS) * BF16_ROWS
    width = H_MOBA * HEAD_DIM
    off = (H_FOX * HEAD_DIM) // width
    return pl.pallas_call(
        functools.partial(_moba_kernel, seq=seq, blk_rows=blk_rows, n_sub=n_sub),
        grid_spec=pltpu.PrefetchScalarGridSpec(
            num_scalar_prefetch=1,
            grid=(bsz, seq // step),
            in_specs=[pl.BlockSpec((None, width, step), lambda b, g, rel: (b, off, g)),
                      pl.BlockSpec((None, seq, width), lambda b, g, rel: (b, 0, off)),
                      pl.BlockSpec((None, width, seq), lambda b, g, rel: (b, off, 0)),
                      pl.BlockSpec((None, step, d), lambda b, g, rel: (b, g, 0)),
                      pl.BlockSpec((None, step, y_c.shape[-1]), lambda b, g, rel: (b, g, 0)),
                      pl.BlockSpec(w_out.shape, lambda b, g, rel: (0, 0),
                                   pipeline_mode=pl.Buffered(1))],
            out_specs=pl.BlockSpec((None, step, d), lambda b, g, rel: (b, g, 0)),
            scratch_shapes=[pltpu.VMEM((blk_rows, width), F32),
                            pltpu.VMEM((H_MOBA, ACC_ROWS, seq), BF16),
                            pltpu.VMEM((H_MOBA, 2, blk, blk), F32),
                            pltpu.VMEM((H_MOBA, 2, blk_rows, blk), F32),
                            pltpu.VMEM((H_MOBA, 1, blk), F32),
                            pltpu.VMEM((H_MOBA, ACC_ROWS, blk), F32)]
            + _pipeline_scratch(H_MOBA, blk)),
        out_shape=jax.ShapeDtypeStruct(h.shape, F32),
        compiler_params=_params(("arbitrary", "arbitrary")),
        name="moba_attention_out",
    )(rel_bias, qt, k, vt, h, y_c, w_out.astype(BF16))


def kernel(x, mix_norm_g, ffn_norm_g, final_norm_g, ev_w_in, ev_conv_w, ev_pool_w, ev_pool_scale,
           ev_w_out, od_w_in, od_b_f, od_w_out, rel_bias, ffn_w_in, ffn_conv_w, ffn_conv_b, ffn_w_out):
    depth = mix_norm_g.shape[0]
    h = x
    for layer in range(depth):
        if layer % 2 == 0:
            e = layer // 2
            h = _even_layer(h, mix_norm_g[layer], ev_w_in[e], ev_conv_w[e], ev_pool_w[e],
                            ev_pool_scale[e], ev_w_out[e])
        else:
            o = layer // 2
            qt, k, vt, zf = _qkv_proj(h, mix_norm_g[layer], od_w_in[o])
            faug = _fgate_cumsum(zf, od_b_f[o])
            y_c = _fox_attention(qt, k, vt, faug)
            h = _moba_attention_out(qt, k, vt, rel_bias, h, y_c, od_w_out[o])
        h = _ffn_layer(h, ffn_norm_g[layer], ffn_w_in[layer], ffn_conv_w[layer], ffn_conv_b[layer],
                       ffn_w_out[layer], final_norm_g, final_norm=(layer == depth - 1))
    return h
```

```python
import functools
import math

import jax
import jax.numpy as jnp
from jax import lax
from jax.experimental import pallas as pl
from jax.experimental.pallas import tpu as pltpu

F32 = jnp.float32
BF16 = jnp.bfloat16

D_MODEL = 1024
HEAD_DIM = 64
CONV_W = 3
A_W = D_MODEL // 2
B_W = D_MODEL - A_W
POOL_WINDOWS = (2, 4, 8, 16)
POOL_G = B_W // len(POOL_WINDOWS)
H_FOX = D_MODEL // (2 * HEAD_DIM)
H_MOBA = D_MODEL // (2 * HEAD_DIM)
ATT_W = (H_FOX + H_MOBA) * HEAD_DIM
ATTN_SCALE = HEAD_DIM ** -0.5
MOBA_BLOCK = 256
MOBA_TOPK = 3
T5_BUCKETS = 32
T5_MAX_DIST = 128
D_FF = 2816
RMS_EPS = 1e-6

LANES = 128
SUBLANES = 8
BF16_ROWS = 16
VMEM_LIMIT_BYTES = 56 * 1024 * 1024

HEAD_PAIR_W = 2 * HEAD_DIM
N_PAIR = H_FOX // 2
MASK_VALUE = -1e30
LOG2E = math.log2(math.e)
F_PIECES = 3

TOKEN_TILE = 512
FFN_CHUNK = D_FF
ATT_TILE = MOBA_BLOCK
ATT_SUBTILES = 2
POOL_HALO = 32


def _rmsnorm(x, g):
    ms = jnp.mean(x * x, axis=-1, keepdims=True)
    return x * lax.rsqrt(ms + RMS_EPS) * g


def _const_spec(shape):
    nd = len(shape)
    return pl.BlockSpec(shape, lambda *_: (0,) * nd, pipeline_mode=pl.Buffered(1))


def _params(semantics):
    return pltpu.CompilerParams(dimension_semantics=semantics, vmem_limit_bytes=VMEM_LIMIT_BYTES)


def _even_kernel(h_ref, g_ref, win_ref, cw_ref, pw_ref, ps_ref, wout_ref, o_ref,
                 cv_sc, u_sc, lvl_sc, tailcv_sc, tailu_sc, *, tile):
    s = pl.program_id(1)
    halo = POOL_HALO

    @pl.when(s == 0)
    def _():
        tailcv_sc[...] = jnp.zeros_like(tailcv_sc)
        tailu_sc[...] = jnp.zeros_like(tailu_sc)

    h = h_ref[...]
    hn = _rmsnorm(h, g_ref[...]).astype(BF16)
    z = jnp.dot(hn, win_ref[...], preferred_element_type=F32)
    gate_b = z[:, :A_W]
    cv = z[:, A_W:2 * A_W] * z[:, 2 * A_W:3 * A_W]
    pool_in = z[:, 3 * A_W:]

    cv_sc[0:SUBLANES, :] = tailcv_sc[...]
    cv_sc[SUBLANES:SUBLANES + tile, :] = cv
    tailcv_sc[...] = cv[tile - SUBLANES:, :]
    conv = (cw_ref[0:1, :] * cv_sc[SUBLANES - 2:SUBLANES - 2 + tile, :]
            + cw_ref[1:2, :] * cv_sc[SUBLANES - 1:SUBLANES - 1 + tile, :]
            + cw_ref[2:3, :] * cv)
    y_a = gate_b * conv

    u_sc[0:halo, :] = tailu_sc[...]
    u_sc[halo:halo + tile, :] = pool_in
    tailu_sc[...] = pool_in[tile - halo:, :]
    pos = s * tile + lax.broadcasted_iota(jnp.int32, (tile, 1), 0)
    ys = [y_a.astype(BF16)]
    for g, w in enumerate(POOL_WINDOWS):
        cols = slice(g * POOL_G, (g + 1) * POOL_G)
        n_lvl = g + 1
        src, src_cols = u_sc, cols
        for lvl in range(1, n_lvl):
            sh = 2 ** (lvl - 1)
            lo = SUBLANES * lvl
            dst = lvl_sc.at[lvl % 2]
            dst[lo:halo + tile, :] = (src[lo:halo + tile, src_cols]
                                      + src[lo - sh:halo + tile - sh, src_cols])
            src, src_cols = dst, slice(None)
        sh = 2 ** (n_lvl - 1)
        wsum = src[halo:halo + tile, src_cols] + src[halo - sh:halo + tile - sh, src_cols]
        cnt = jnp.minimum(pos + 1, w).astype(F32)
        p = wsum / cnt - pool_in[:, cols]
        yb = jnp.dot(p.astype(BF16), pw_ref[g], preferred_element_type=F32) * ps_ref[:, cols]
        ys.append(yb.astype(BF16))
    y = jnp.concatenate(ys, axis=-1)
    o_ref[...] = h + jnp.dot(y, wout_ref[...], preferred_element_type=F32)


def _even_layer(h, g, w_in, conv_w, pool_w, pool_scale, w_out):
    bsz, seq, d = h.shape
    tile = min(2 * TOKEN_TILE, seq)
    assert seq % tile == 0 and tile >= POOL_HALO
    act_spec = pl.BlockSpec((None, tile, d), lambda b, s: (b, s, 0))
    return pl.pallas_call(
        functools.partial(_even_kernel, tile=tile),
        grid=(bsz, seq // tile),
        in_specs=[act_spec, _const_spec((1, d)), _const_spec(w_in.shape), _const_spec(conv_w.shape),
                  _const_spec(pool_w.shape), _const_spec((1, B_W)), _const_spec(w_out.shape)],
        out_specs=act_spec,
        out_shape=jax.ShapeDtypeStruct(h.shape, F32),
        scratch_shapes=[pltpu.VMEM((SUBLANES + tile, A_W), F32),
                        pltpu.VMEM((POOL_HALO + tile, B_W), F32),
                        pltpu.VMEM((2, POOL_HALO + tile, POOL_G), F32),
                        pltpu.VMEM((SUBLANES, A_W), F32),
                        pltpu.VMEM((POOL_HALO, B_W), F32)],
        compiler_params=_params(("arbitrary", "arbitrary")),
        name="even_mixer",
    )(h, g.reshape(1, d), w_in.astype(BF16), conv_w, pool_w.astype(BF16),
      pool_scale.reshape(1, B_W), w_out.astype(BF16))


def _ffn_kernel(h_ref, g_ref, win_ref, cw_ref, cb_ref, wout_ref, fg_ref, o_ref, ext_sc, tail_sc,
                *, tile, final_norm):
    s = pl.program_id(1)

    @pl.when(s == 0)
    def _():
        tail_sc[...] = jnp.zeros_like(tail_sc)

    h = h_ref[...]
    hn = _rmsnorm(h, g_ref[...]).astype(BF16)
    acc = h
    for c in range(D_FF // FFN_CHUNK):
        cols = slice(c * FFN_CHUNK, (c + 1) * FFN_CHUNK)
        gcols = slice(D_FF + c * FFN_CHUNK, D_FF + (c + 1) * FFN_CHUNK)
        u = jnp.dot(hn, win_ref[:, cols], preferred_element_type=F32)
        gt = jnp.dot(hn, win_ref[:, gcols], preferred_element_type=F32)
        ext_sc[0:SUBLANES, :] = tail_sc[:, cols]
        ext_sc[SUBLANES:SUBLANES + tile, :] = u
        tail_sc[:, cols] = u[tile - SUBLANES:, :]
        a = (cw_ref[0:1, cols] * ext_sc[SUBLANES - 2:SUBLANES - 2 + tile, :]
             + cw_ref[1:2, cols] * ext_sc[SUBLANES - 1:SUBLANES - 1 + tile, :]
             + cw_ref[2:3, cols] * u
             + cb_ref[:, cols])
        act = a * (1.0 / (1.0 + jnp.exp(-a))) * gt
        acc = acc + jnp.dot(act.astype(BF16), wout_ref[cols, :], preferred_element_type=F32)
    if final_norm:
        acc = _rmsnorm(acc, fg_ref[...])
    o_ref[...] = acc


def _ffn_layer(h, g, w_in, conv_w, conv_b, w_out, final_g, final_norm):
    bsz, seq, d = h.shape
    tile = min(TOKEN_TILE, seq)
    assert seq % tile == 0
    act_spec = pl.BlockSpec((None, tile, d), lambda b, s: (b, s, 0))
    return pl.pallas_call(
        functools.partial(_ffn_kernel, tile=tile, final_norm=final_norm),
        grid=(bsz, seq // tile),
        in_specs=[act_spec, _const_spec((1, d)), _const_spec(w_in.shape), _const_spec(conv_w.shape),
                  _const_spec((1, D_FF)), _const_spec(w_out.shape), _const_spec((1, d))],
        out_specs=act_spec,
        out_shape=jax.ShapeDtypeStruct(h.shape, F32),
        scratch_shapes=[pltpu.VMEM((SUBLANES + tile, FFN_CHUNK), F32),
                        pltpu.VMEM((SUBLANES, D_FF), F32)],
        compiler_params=_params(("arbitrary", "arbitrary")),
        name="conv_ffn",
    )(h, g.reshape(1, d), w_in.astype(BF16), conv_w, conv_b.reshape(1, D_FF), w_out.astype(BF16),
      final_g.reshape(1, d))


_NT_DIMS = (((1,), (1,)), ((), ()))


def _qkv_kernel(h_ref, g_ref, wqt_ref, wk_ref, wvt_ref, wf_ref, qt_ref, k_ref, vt_ref, zf_ref):
    hn = _rmsnorm(h_ref[...], g_ref[...]).astype(BF16)
    qt = lax.dot_general(wqt_ref[...], hn, _NT_DIMS, preferred_element_type=F32)
    qt_ref[...] = (qt * (ATTN_SCALE * LOG2E)).astype(BF16)
    k_ref[...] = jnp.dot(hn, wk_ref[...], preferred_element_type=F32).astype(BF16)
    vt_ref[...] = lax.dot_general(wvt_ref[...], hn, _NT_DIMS, preferred_element_type=F32).astype(BF16)
    zf_ref[...] = jnp.dot(hn, wf_ref[...], preferred_element_type=F32)


def _qkv_proj(h, g, w_in):
    bsz, seq, d = h.shape
    tile = min(2 * TOKEN_TILE, seq)
    assert seq % tile == 0
    w_qt = w_in[:, :ATT_W].T.astype(BF16)
    w_k = w_in[:, ATT_W:2 * ATT_W].astype(BF16)
    w_vt = w_in[:, 2 * ATT_W:3 * ATT_W].T.astype(BF16)
    w_f = jnp.pad(w_in[:, 3 * ATT_W:], ((0, 0), (0, LANES - H_FOX))).astype(BF16)
    act_spec = pl.BlockSpec((None, tile, d), lambda b, s: (b, s, 0))
    tok_major = pl.BlockSpec((None, tile, ATT_W), lambda b, s: (b, s, 0))
    feat_major = pl.BlockSpec((None, ATT_W, tile), lambda b, s: (b, 0, s))
    zf_spec = pl.BlockSpec((None, tile, LANES), lambda b, s: (b, s, 0))
    return pl.pallas_call(
        _qkv_kernel,
        grid=(bsz, seq // tile),
        in_specs=[act_spec, _const_spec((1, d)), _const_spec(w_qt.shape), _const_spec(w_k.shape),
                  _const_spec(w_vt.shape), _const_spec(w_f.shape)],
        out_specs=[feat_major, tok_major, feat_major, zf_spec],
        out_shape=[jax.ShapeDtypeStruct((bsz, ATT_W, seq), BF16),
                   jax.ShapeDtypeStruct((bsz, seq, ATT_W), BF16),
                   jax.ShapeDtypeStruct((bsz, ATT_W, seq), BF16),
                   jax.ShapeDtypeStruct((bsz, seq, LANES), F32)],
        compiler_params=_params(("arbitrary", "arbitrary")),
        name="qkv_proj",
    )(h, g.reshape(1, d), w_qt, w_k, w_vt, w_f)


def _fgate_kernel(zf_ref, bf_ref, o_ref, *, seq):
    z = zf_ref[...] + bf_ref[...]
    x = -(jnp.maximum(-z, 0.0) + jnp.log1p(jnp.exp(-jnp.abs(z))))
    row = lax.broadcasted_iota(jnp.int32, (seq, LANES), 0)
    sh = 1
    while sh < seq:
        x = x + jnp.where(row >= sh, pltpu.roll(x, sh, 0), 0.0)
        sh *= 2
    x = x * LOG2E
    pieces = []
    for _ in range(F_PIECES):
        piece = x.astype(BF16)
        pieces.append(piece)
        x = x - piece.astype(F32)
    src = lax.broadcasted_iota(jnp.int32, (F_PIECES * LANES, N_PAIR * LANES), 0)
    dst = lax.broadcasted_iota(jnp.int32, (F_PIECES * LANES, N_PAIR * LANES), 1)
    piece_id, head = src // LANES, src % LANES
    pair, slot = dst // LANES, dst % LANES
    hit = (head // 2 == pair) & (slot == F_PIECES * (head % 2) + piece_id) & (head < H_FOX)
    place = jnp.where(hit, 1.0, 0.0).astype(BF16)
    o_ref[...] = jnp.dot(jnp.concatenate(pieces, axis=-1), place,
                         preferred_element_type=F32).astype(BF16)


def _fgate_cumsum(zf, b_f):
    bsz, seq, _ = zf.shape
    return pl.pallas_call(
        functools.partial(_fgate_kernel, seq=seq),
        grid=(bsz,),
        in_specs=[pl.BlockSpec((None, seq, LANES), lambda b: (b, 0, 0)), _const_spec((1, LANES))],
        out_specs=pl.BlockSpec((None, seq, N_PAIR * LANES), lambda b: (b, 0, 0)),
        out_shape=jax.ShapeDtypeStruct((bsz, seq, N_PAIR * LANES), BF16),
        compiler_params=_params(("arbitrary",)),
        name="fgate_cumsum",
    )(zf, jnp.pad(b_f, (0, LANES - H_FOX)).reshape(1, LANES))


ACC_ROWS = HEAD_DIM + BF16_ROWS


def _softmax_init(m_sc, acc_sc):
    m_sc[...] = jnp.full_like(m_sc, MASK_VALUE)
    acc_sc[...] = jnp.zeros_like(acc_sc)


def _fill_value_rows(vaug_sc, vt_ref, n_heads):
    ones = jnp.ones((BF16_ROWS, vt_ref.shape[1]), BF16)
    for h in range(n_heads):
        vaug_sc[h, 0:HEAD_DIM, :] = vt_ref[h * HEAD_DIM:(h + 1) * HEAD_DIM, :]
        vaug_sc[h, HEAD_DIM:, :] = ones


def _flash_pipeline(n_tiles, first, second, rest, key_start, tk, n_heads,
                    vaug_sc, slots, m_sc, acc_sc):
    def stage_q(tiles, slot):
        s_sc, t_sc, _, _ = slots[slot]
        for h in range(n_heads):
            s_sc[h] = tiles[h]
            t_sc[h] = jnp.max(tiles[h], axis=0, keepdims=True)

    def stage_x(slot):
        s_sc, t_sc, p_sc, a_sc = slots[slot]
        for h in range(n_heads):
            m_prev = m_sc[h]
            m_new = jnp.maximum(m_prev, t_sc[h])
            a_sc[h] = jnp.exp2(m_prev - m_new)
            p_sc[h] = jnp.exp2(s_sc[h] - m_new).astype(BF16)
            m_sc[h] = m_new

    def stage_v(u, slot):
        _, _, p_sc, a_sc = slots[slot]
        k0 = key_start(u)
        for h in range(n_heads):
            pv = jnp.dot(vaug_sc[h, :, pl.ds(k0, tk)], p_sc[h], preferred_element_type=F32)
            acc_sc[h] = a_sc[h] * acc_sc[h] + pv

    def step(u, slot):
        stage_q(rest(u + 1), 1 - slot)
        stage_v(u - 1, 1 - slot)
        stage_x(slot)

    n_eff = jnp.maximum(n_tiles, 2)
    _softmax_init(m_sc, acc_sc)
    stage_q(first(), 0)
    stage_q(second(), 1)
    stage_x(0)

    n_steps = n_eff - 2

    def pair_body(k, carry):
        u = 2 * k + 1
        step(u, 1)
        step(u + 1, 0)
        return carry

    lax.fori_loop(0, n_steps // 2, pair_body, 0)

    @pl.when(n_steps % 2 == 1)
    def _():
        step(n_eff - 2, 1)

    for parity in range(2):
        @pl.when(n_eff % 2 == parity)
        def _():
            stage_v(n_eff - 2, parity)
            stage_x(1 - parity)
            stage_v(n_eff - 1, 1 - parity)


def _pipeline_scratch(n_heads, tile):
    one = [pltpu.VMEM((n_heads, tile, tile), F32), pltpu.VMEM((n_heads, 1, tile), F32),
           pltpu.VMEM((n_heads, tile, tile), BF16), pltpu.VMEM((n_heads, 1, tile), F32)]
    return one + one


def _softmax_finish(acc_sc, n_heads):
    blocks = []
    for pair in range(n_heads // 2):
        halves = []
        for h in (2 * pair, 2 * pair + 1):
            acc = acc_sc[h]
            halves.append(acc[0:HEAD_DIM, :] / acc[HEAD_DIM:HEAD_DIM + 1, :])
        out_t = jnp.concatenate(halves, axis=0)
        blocks.append(jnp.transpose(out_t).astype(BF16))
    return blocks


def _head_rows(qt_ref, h, q0, tq):
    pair, e = divmod(h, 2)
    q = qt_ref[pair * HEAD_PAIR_W:(pair + 1) * HEAD_PAIR_W, pl.ds(q0, tq)]
    row = lax.broadcasted_iota(jnp.int32, q.shape, 0)
    mine = (row >= e * HEAD_DIM) & (row < (e + 1) * HEAD_DIM)
    return jnp.where(mine, q, jnp.zeros_like(q))


def _key_le_query(tile):
    key = lax.broadcasted_iota(jnp.int32, (tile, tile), 0)
    query = lax.broadcasted_iota(jnp.int32, (tile, tile), 1)
    return key <= query


def _fox_kernel(qt_ref, k_ref, f_ref, vt_ref, o_ref,
                kaug_sc, vaug_sc, m_sc, acc_sc, *slot_refs, tile, n_sub):
    g = pl.program_id(1)

    @pl.when(g == 0)
    def _():
        for pair in range(N_PAIR):
            kaug_sc[pair, :, 0:HEAD_PAIR_W] = k_ref[:, pair * HEAD_PAIR_W:(pair + 1) * HEAD_PAIR_W]
            kaug_sc[pair, :, HEAD_PAIR_W:] = f_ref[:, pair * LANES:(pair + 1) * LANES]
        _fill_value_rows(vaug_sc, vt_ref, H_FOX)

    def query_tile(sub, carry):
        i = g * n_sub + sub
        q0 = pl.multiple_of(sub * tile, tile)
        row = lax.broadcasted_iota(jnp.int32, (LANES, tile), 0)
        q_aug = []
        for h in range(H_FOX):
            e = h % 2
            minus_one = (row >= F_PIECES * e) & (row < F_PIECES * (e + 1))
            sel = jnp.where(minus_one, -1.0, 0.0).astype(BF16)
            q_aug.append(jnp.concatenate([_head_rows(qt_ref, h, q0, tile), sel], axis=0))

        def key_start(u):
            j = jnp.clip(jnp.where(u == 0, i, u - 1), 0, i)
            return pl.multiple_of(j * tile, tile)

        def scores(u):
            k0 = key_start(u)
            return [jnp.dot(kaug_sc[h // 2, pl.ds(k0, tile), :], q_aug[h],
                            preferred_element_type=F32) for h in range(H_FOX)]

        def first():
            allowed = _key_le_query(tile)
            return [jnp.where(allowed, s_t, MASK_VALUE) for s_t in scores(0)]

        def second():
            absent = jnp.where(i >= 1, 0.0, MASK_VALUE)
            return [s_t + absent for s_t in scores(1)]

        _flash_pipeline(i + 1, first, second, scores, key_start, tile, H_FOX,
                        vaug_sc, (slot_refs[:4], slot_refs[4:]), m_sc, acc_sc)
        for pair, block in enumerate(_softmax_finish(acc_sc, H_FOX)):
            o_ref[pl.ds(q0, tile), pair * HEAD_PAIR_W:(pair + 1) * HEAD_PAIR_W] = block
        return carry

    lax.fori_loop(0, n_sub, query_tile, 0)


def _fox_attention(qt, k, vt, faug):
    bsz, seq, _ = k.shape
    tile = min(ATT_TILE, seq)
    n_sub = min(ATT_SUBTILES, seq // tile)
    step = tile * n_sub
    assert seq % step == 0 and tile % LANES == 0
    width = H_FOX * HEAD_DIM
    return pl.pallas_call(
        functools.partial(_fox_kernel, tile=tile, n_sub=n_sub),
        grid=(bsz, seq // step),
        in_specs=[pl.BlockSpec((None, width, step), lambda b, g: (b, 0, g)),
                  pl.BlockSpec((None, seq, width), lambda b, g: (b, 0, 0)),
                  pl.BlockSpec((None, seq, N_PAIR * LANES), lambda b, g: (b, 0, 0)),
                  pl.BlockSpec((None, width, seq), lambda b, g: (b, 0, 0))],
        out_specs=pl.BlockSpec((None, step, width), lambda b, g: (b, g, 0)),
        out_shape=jax.ShapeDtypeStruct((bsz, seq, width), BF16),
        scratch_shapes=[pltpu.VMEM((N_PAIR, seq, 2 * HEAD_PAIR_W), BF16),
                        pltpu.VMEM((H_FOX, ACC_ROWS, seq), BF16),
                        pltpu.VMEM((H_FOX, 1, tile), F32),
                        pltpu.VMEM((H_FOX, ACC_ROWS, tile), F32)] + _pipeline_scratch(H_FOX, tile),
        compiler_params=_params(("arbitrary", "arbitrary")),
        name="fox_attention",
    )(qt, k, faug, vt)


def _t5_bias_tile(dist, rel_ref, head):
    exact = T5_BUCKETS // 2
    d_f = jnp.maximum(dist, 1).astype(F32)
    log_b = exact + (jnp.log(d_f / exact) / math.log(T5_MAX_DIST / exact)
                     * (T5_BUCKETS - exact)).astype(jnp.int32)
    log_b = jnp.minimum(log_b, T5_BUCKETS - 1)
    bucket = jnp.where(dist < exact, dist, log_b)
    out = jnp.zeros(dist.shape, F32)
    for b in range(T5_BUCKETS):
        out = jnp.where(bucket == b, rel_ref[b, head], out)
    return out


def _moba_kernel(rel_ref, qt_ref, k_ref, vt_ref, h_ref, yc_ref, wo_ref, o_ref,
                 kmean_sc, vaug_sc, bias_sc, pen_sc, m_sc, acc_sc, *slot_refs,
                 seq, blk_rows, n_sub):
    blk = MOBA_BLOCK
    b = pl.program_id(0)
    g = pl.program_id(1)

    @pl.when((b == 0) & (g == 0))
    def _():
        key = lax.broadcasted_iota(jnp.int32, (blk, blk), 0)
        query = lax.broadcasted_iota(jnp.int32, (blk, blk), 1)
        for h in range(H_MOBA):
            for delta in range(2):
                dist = jnp.maximum(delta * blk + query - key, 0)
                bias_sc[h, delta] = _t5_bias_tile(dist, rel_ref, h) * LOG2E

    @pl.when(g == 0)
    def _():
        mrow = lax.broadcasted_iota(jnp.int32, (blk_rows, seq), 0)
        mcol = lax.broadcasted_iota(jnp.int32, (blk_rows, seq), 1) // blk
        avg = jnp.where(mrow == mcol, 1.0 / blk, 0.0).astype(BF16)
        kmean_sc[...] = jnp.dot(avg, k_ref[...], preferred_element_type=F32)
        _fill_value_rows(vaug_sc, vt_ref, H_MOBA)

    def query_tile(sub, carry):
        i = g * n_sub + sub
        q0 = pl.multiple_of(sub * blk, blk)
        qh = [_head_rows(qt_ref, h, q0, blk) for h in range(H_MOBA)]

        blk_id = lax.broadcasted_iota(jnp.int32, (blk_rows, blk), 0)
        q_far = []
        for h in range(H_MOBA):
            pair = h // 2
            kmean = kmean_sc[:, pair * HEAD_PAIR_W:(pair + 1) * HEAD_PAIR_W]
            km_hi = kmean.astype(BF16)
            km_lo = (kmean - km_hi.astype(F32)).astype(BF16)
            gate = (jnp.dot(km_hi, qh[h], preferred_element_type=F32)
                    + jnp.dot(km_lo, qh[h], preferred_element_type=F32))
            gate = jnp.where(blk_id < i, gate, -jnp.inf)
            sel = jnp.zeros(gate.shape, jnp.bool_)
            for _ in range(MOBA_TOPK):
                best = jnp.max(gate, axis=0, keepdims=True)
                first_id = jnp.min(jnp.where(gate == best, blk_id, blk_rows), axis=0, keepdims=True)
                pick = (blk_id == first_id) & (best > -jnp.inf)
                sel = sel | pick
                gate = jnp.where(pick, -jnp.inf, gate)
            pen = jnp.where(sel, 0.0, MASK_VALUE)
            pen_sc[h] = pen
            far = pen + rel_ref[T5_BUCKETS - 1, h] * LOG2E
            far_hi = far.astype(BF16)
            far_lo = (far - far_hi.astype(F32)).astype(BF16)
            pad = jnp.zeros((HEAD_PAIR_W - 2 * blk_rows, blk), BF16)
            q_far.append(jnp.concatenate([qh[h], far_hi, far_lo, pad], axis=0))

        def key_block(u):
            return jnp.clip(jnp.where(u == 0, i, jnp.where(u == 1, i - 1, u - 2)), 0, i)

        def key_start(u):
            return pl.multiple_of(key_block(u) * blk, blk)

        def scores(h, u):
            pair = h // 2
            k = k_ref[pl.ds(key_start(u), blk), pair * HEAD_PAIR_W:(pair + 1) * HEAD_PAIR_W]
            return jnp.dot(k, qh[h], preferred_element_type=F32)

        def first():
            allowed = _key_le_query(blk)
            return [jnp.where(allowed, scores(h, 0) + bias_sc[h, 0], MASK_VALUE)
                    for h in range(H_MOBA)]

        def second():
            j = key_block(1)
            return [scores(h, 1) + bias_sc[h, 1] + pen_sc[h, pl.ds(j, 1), :]
                    for h in range(H_MOBA)]

        def rest(u):
            j = key_block(u)
            lane = lax.broadcasted_iota(jnp.int32, (blk, HEAD_PAIR_W), 1)
            one_hot = jnp.where((lane == j) | (lane == blk_rows + j), 1.0, 0.0).astype(BF16)
            tiles = []
            for h in range(H_MOBA):
                pair = h // 2
                k = k_ref[pl.ds(key_start(u), blk), pair * HEAD_PAIR_W:(pair + 1) * HEAD_PAIR_W]
                k_far = jnp.concatenate([k, one_hot], axis=1)
                tiles.append(jnp.dot(k_far, q_far[h], preferred_element_type=F32))
            return tiles

        _flash_pipeline(i + 1, first, second, rest, key_start, blk, H_MOBA,
                        vaug_sc, (slot_refs[:4], slot_refs[4:]), m_sc, acc_sc)

        y_d = jnp.concatenate(_softmax_finish(acc_sc, H_MOBA), axis=-1)
        wc = yc_ref.shape[-1]
        rows = pl.ds(q0, blk)
        o_ref[rows, :] = (h_ref[rows, :]
                          + jnp.dot(yc_ref[rows, :], wo_ref[0:wc, :], preferred_element_type=F32)
                          + jnp.dot(y_d, wo_ref[wc:, :], preferred_element_type=F32))
        return carry

    lax.fori_loop(0, n_sub, query_tile, 0)


def _moba_attention_out(qt, k, vt, rel_bias, h, y_c, w_out):
    bsz, seq, d = h.shape
    blk = MOBA_BLOCK
    n_sub = min(ATT_SUBTILES, seq // blk)
    step = blk * n_sub
    assert seq % step == 0
    assert blk + 1 >= T5_MAX_DIST
    blk_rows = -(-(seq // blk) // BF16_ROWS) * BF16_ROWS
    width = H_MOBA * HEAD_DIM
    off = (H_FOX * HEAD_DIM) // width
    return pl.pallas_call(
        functools.partial(_moba_kernel, seq=seq, blk_rows=blk_rows, n_sub=n_sub),
        grid_spec=pltpu.PrefetchScalarGridSpec(
            num_scalar_prefetch=1,
            grid=(bsz, seq // step),
            in_specs=[pl.BlockSpec((None, width, step), lambda b, g, rel: (b, off, g)),
                      pl.BlockSpec((None, seq, width), lambda b, g, rel: (b, 0, off)),
                      pl.BlockSpec((None, width, seq), lambda b, g, rel: (b, off, 0)),
                      pl.BlockSpec((None, step, d), lambda b, g, rel: (b, g, 0)),
                      pl.BlockSpec((None, step, y_c.shape[-1]), lambda b, g, rel: (b, g, 0)),
                      pl.BlockSpec(w_out.shape, lambda b, g, rel: (0, 0),
                                   pipeline_mode=pl.Buffered(1))],
            out_specs=pl.BlockSpec((None, step, d), lambda b, g, rel: (b, g, 0)),
            scratch_shapes=[pltpu.VMEM((blk_rows, width), F32),
                            pltpu.VMEM((H_MOBA, ACC_ROWS, seq), BF16),
                            pltpu.VMEM((H_MOBA, 2, blk, blk), F32),
                            pltpu.VMEM((H_MOBA, blk_rows, blk), F32),
                            pltpu.VMEM((H_MOBA, 1, blk), F32),
                            pltpu.VMEM((H_MOBA, ACC_ROWS, blk), F32)]
            + _pipeline_scratch(H_MOBA, blk)),
        out_shape=jax.ShapeDtypeStruct(h.shape, F32),
        compiler_params=_params(("arbitrary", "arbitrary")),
        name="moba_attention_out",
    )(rel_bias, qt, k, vt, h, y_c, w_out.astype(BF16))


def kernel(x, mix_norm_g, ffn_norm_g, final_norm_g, ev_w_in, ev_conv_w, ev_pool_w, ev_pool_scale,
           ev_w_out, od_w_in, od_b_f, od_w_out, rel_bias, ffn_w_in, ffn_conv_w, ffn_conv_b, ffn_w_out):
    depth = mix_norm_g.shape[0]
    h = x
    for layer in range(depth):
        if layer % 2 == 0:
            e = layer // 2
            h = _even_layer(h, mix_norm_g[layer], ev_w_in[e], ev_conv_w[e], ev_pool_w[e],
                            ev_pool_scale[e], ev_w_out[e])
        else:
            o = layer // 2
            qt, k, vt, zf = _qkv_proj(h, mix_norm_g[layer], od_w_in[o])
            faug = _fgate_cumsum(zf, od_b_f[o])
            y_c = _fox_attention(qt, k, vt, faug)
            h = _moba_attention_out(qt, k, vt, rel_bias, h, y_c, od_w_out[o])
        h = _ffn_layer(h, ffn_norm_g[layer], ffn_w_in[layer], ffn_conv_w[layer], ffn_conv_b[layer],
                       ffn_w_out[layer], final_norm_g, final_norm=(layer == depth - 1))
    return h
```

```python
import functools
import math

import jax
import jax.numpy as jnp
from jax import lax
from jax.experimental import pallas as pl
from jax.experimental.pallas import tpu as pltpu

F32 = jnp.float32
BF16 = jnp.bfloat16

D_MODEL = 1024
HEAD_DIM = 64
CONV_W = 3
A_W = D_MODEL // 2
B_W = D_MODEL - A_W
POOL_WINDOWS = (2, 4, 8, 16)
POOL_G = B_W // len(POOL_WINDOWS)
H_FOX = D_MODEL // (2 * HEAD_DIM)
H_MOBA = D_MODEL // (2 * HEAD_DIM)
ATT_W = (H_FOX + H_MOBA) * HEAD_DIM
ATTN_SCALE = HEAD_DIM ** -0.5
MOBA_BLOCK = 256
MOBA_TOPK = 3
T5_BUCKETS = 32
T5_MAX_DIST = 128
D_FF = 2816
RMS_EPS = 1e-6

LANES = 128
SUBLANES = 8
BF16_ROWS = 16
VMEM_LIMIT_BYTES = 56 * 1024 * 1024

HEAD_PAIR_W = 2 * HEAD_DIM
N_PAIR = H_FOX // 2
MASK_VALUE = -1e30
LOG2E = math.log2(math.e)
F_PIECES = 3

TOKEN_TILE = 512
FFN_CHUNK = D_FF
ATT_TILE = MOBA_BLOCK
POOL_HALO = 32


def _rmsnorm(x, g):
    ms = jnp.mean(x * x, axis=-1, keepdims=True)
    return x * lax.rsqrt(ms + RMS_EPS) * g


def _const_spec(shape):
    nd = len(shape)
    return pl.BlockSpec(shape, lambda *_: (0,) * nd, pipeline_mode=pl.Buffered(1))


def _params(semantics):
    return pltpu.CompilerParams(dimension_semantics=semantics, vmem_limit_bytes=VMEM_LIMIT_BYTES)


def _even_kernel(h_ref, g_ref, win_ref, cw_ref, pw_ref, ps_ref, wout_ref, o_ref,
                 cv_sc, u_sc, lvl_sc, tailcv_sc, tailu_sc, *, tile):
    s = pl.program_id(1)
    halo = POOL_HALO

    @pl.when(s == 0)
    def _():
        tailcv_sc[...] = jnp.zeros_like(tailcv_sc)
        tailu_sc[...] = jnp.zeros_like(tailu_sc)

    h = h_ref[...]
    hn = _rmsnorm(h, g_ref[...]).astype(BF16)
    z = jnp.dot(hn, win_ref[...], preferred_element_type=F32)
    gate_b = z[:, :A_W]
    cv = z[:, A_W:2 * A_W] * z[:, 2 * A_W:3 * A_W]
    pool_in = z[:, 3 * A_W:]

    cv_sc[0:SUBLANES, :] = tailcv_sc[...]
    cv_sc[SUBLANES:SUBLANES + tile, :] = cv
    tailcv_sc[...] = cv[tile - SUBLANES:, :]
    conv = (cw_ref[0:1, :] * cv_sc[SUBLANES - 2:SUBLANES - 2 + tile, :]
            + cw_ref[1:2, :] * cv_sc[SUBLANES - 1:SUBLANES - 1 + tile, :]
            + cw_ref[2:3, :] * cv)
    y_a = gate_b * conv

    u_sc[0:halo, :] = tailu_sc[...]
    u_sc[halo:halo + tile, :] = pool_in
    tailu_sc[...] = pool_in[tile - halo:, :]
    pos = s * tile + lax.broadcasted_iota(jnp.int32, (tile, 1), 0)
    ys = [y_a.astype(BF16)]
    for g, w in enumerate(POOL_WINDOWS):
        cols = slice(g * POOL_G, (g + 1) * POOL_G)
        n_lvl = g + 1
        src, src_cols = u_sc, cols
        for lvl in range(1, n_lvl):
            sh = 2 ** (lvl - 1)
            lo = SUBLANES * lvl
            dst = lvl_sc.at[lvl % 2]
            dst[lo:halo + tile, :] = (src[lo:halo + tile, src_cols]
                                      + src[lo - sh:halo + tile - sh, src_cols])
            src, src_cols = dst, slice(None)
        sh = 2 ** (n_lvl - 1)
        wsum = src[halo:halo + tile, src_cols] + src[halo - sh:halo + tile - sh, src_cols]
        cnt = jnp.minimum(pos + 1, w).astype(F32)
        p = wsum / cnt - pool_in[:, cols]
        yb = jnp.dot(p.astype(BF16), pw_ref[g], preferred_element_type=F32) * ps_ref[:, cols]
        ys.append(yb.astype(BF16))
    y = jnp.concatenate(ys, axis=-1)
    o_ref[...] = h + jnp.dot(y, wout_ref[...], preferred_element_type=F32)


def _even_layer(h, g, w_in, conv_w, pool_w, pool_scale, w_out):
    bsz, seq, d = h.shape
    tile = min(2 * TOKEN_TILE, seq)
    assert seq % tile == 0 and tile >= POOL_HALO
    act_spec = pl.BlockSpec((None, tile, d), lambda b, s: (b, s, 0))
    return pl.pallas_call(
        functools.partial(_even_kernel, tile=tile),
        grid=(bsz, seq // tile),
        in_specs=[act_spec, _const_spec((1, d)), _const_spec(w_in.shape), _const_spec(conv_w.shape),
                  _const_spec(pool_w.shape), _const_spec((1, B_W)), _const_spec(w_out.shape)],
        out_specs=act_spec,
        out_shape=jax.ShapeDtypeStruct(h.shape, F32),
        scratch_shapes=[pltpu.VMEM((SUBLANES + tile, A_W), F32),
                        pltpu.VMEM((POOL_HALO + tile, B_W), F32),
                        pltpu.VMEM((2, POOL_HALO + tile, POOL_G), F32),
                        pltpu.VMEM((SUBLANES, A_W), F32),
                        pltpu.VMEM((POOL_HALO, B_W), F32)],
        compiler_params=_params(("arbitrary", "arbitrary")),
        name="even_mixer",
    )(h, g.reshape(1, d), w_in.astype(BF16), conv_w, pool_w.astype(BF16),
      pool_scale.reshape(1, B_W), w_out.astype(BF16))


def _ffn_kernel(h_ref, g_ref, win_ref, cw_ref, cb_ref, wout_ref, fg_ref, o_ref, ext_sc, tail_sc,
                *, tile, final_norm):
    s = pl.program_id(1)

    @pl.when(s == 0)
    def _():
        tail_sc[...] = jnp.zeros_like(tail_sc)

    h = h_ref[...]
    hn = _rmsnorm(h, g_ref[...]).astype(BF16)
    acc = h
    for c in range(D_FF // FFN_CHUNK):
        cols = slice(c * FFN_CHUNK, (c + 1) * FFN_CHUNK)
        gcols = slice(D_FF + c * FFN_CHUNK, D_FF + (c + 1) * FFN_CHUNK)
        u = jnp.dot(hn, win_ref[:, cols], preferred_element_type=F32)
        gt = jnp.dot(hn, win_ref[:, gcols], preferred_element_type=F32)
        ext_sc[0:SUBLANES, :] = tail_sc[:, cols]
        ext_sc[SUBLANES:SUBLANES + tile, :] = u
        tail_sc[:, cols] = u[tile - SUBLANES:, :]
        a = (cw_ref[0:1, cols] * ext_sc[SUBLANES - 2:SUBLANES - 2 + tile, :]
             + cw_ref[1:2, cols] * ext_sc[SUBLANES - 1:SUBLANES - 1 + tile, :]
             + cw_ref[2:3, cols] * u
             + cb_ref[:, cols])
        act = a * (1.0 / (1.0 + jnp.exp(-a))) * gt
        acc = acc + jnp.dot(act.astype(BF16), wout_ref[cols, :], preferred_element_type=F32)
    if final_norm:
        acc = _rmsnorm(acc, fg_ref[...])
    o_ref[...] = acc


def _ffn_layer(h, g, w_in, conv_w, conv_b, w_out, final_g, final_norm):
    bsz, seq, d = h.shape
    tile = min(TOKEN_TILE, seq)
    assert seq % tile == 0
    act_spec = pl.BlockSpec((None, tile, d), lambda b, s: (b, s, 0))
    return pl.pallas_call(
        functools.partial(_ffn_kernel, tile=tile, final_norm=final_norm),
        grid=(bsz, seq // tile),
        in_specs=[act_spec, _const_spec((1, d)), _const_spec(w_in.shape), _const_spec(conv_w.shape),
                  _const_spec((1, D_FF)), _const_spec(w_out.shape), _const_spec((1, d))],
        out_specs=act_spec,
        out_shape=jax.ShapeDtypeStruct(h.shape, F32),
        scratch_shapes=[pltpu.VMEM((SUBLANES + tile, FFN_CHUNK), F32),
                        pltpu.VMEM((SUBLANES, D_FF), F32)],
        compiler_params=_params(("arbitrary", "arbitrary")),
        name="conv_ffn",
    )(h, g.reshape(1, d), w_in.astype(BF16), conv_w, conv_b.reshape(1, D_FF), w_out.astype(BF16),
      final_g.reshape(1, d))


_NT_DIMS = (((1,), (1,)), ((), ()))


def _qkv_kernel(h_ref, g_ref, wqt_ref, wk_ref, wvt_ref, wf_ref, qt_ref, k_ref, vt_ref, zf_ref):
    hn = _rmsnorm(h_ref[...], g_ref[...]).astype(BF16)
    qt = lax.dot_general(wqt_ref[...], hn, _NT_DIMS, preferred_element_type=F32)
    qt_ref[...] = (qt * (ATTN_SCALE * LOG2E)).astype(BF16)
    k_ref[...] = jnp.dot(hn, wk_ref[...], preferred_element_type=F32).astype(BF16)
    vt_ref[...] = lax.dot_general(wvt_ref[...], hn, _NT_DIMS, preferred_element_type=F32).astype(BF16)
    zf_ref[...] = jnp.dot(hn, wf_ref[...], preferred_element_type=F32)


def _qkv_proj(h, g, w_in):
    bsz, seq, d = h.shape
    tile = min(2 * TOKEN_TILE, seq)
    assert seq % tile == 0
    w_qt = w_in[:, :ATT_W].T.astype(BF16)
    w_k = w_in[:, ATT_W:2 * ATT_W].astype(BF16)
    w_vt = w_in[:, 2 * ATT_W:3 * ATT_W].T.astype(BF16)
    w_f = jnp.pad(w_in[:, 3 * ATT_W:], ((0, 0), (0, LANES - H_FOX))).astype(BF16)
    act_spec = pl.BlockSpec((None, tile, d), lambda b, s: (b, s, 0))
    tok_major = pl.BlockSpec((None, tile, ATT_W), lambda b, s: (b, s, 0))
    feat_major = pl.BlockSpec((None, ATT_W, tile), lambda b, s: (b, 0, s))
    zf_spec = pl.BlockSpec((None, tile, LANES), lambda b, s: (b, s, 0))
    return pl.pallas_call(
        _qkv_kernel,
        grid=(bsz, seq // tile),
        in_specs=[act_spec, _const_spec((1, d)), _const_spec(w_qt.shape), _const_spec(w_k.shape),
                  _const_spec(w_vt.shape), _const_spec(w_f.shape)],
        out_specs=[feat_major, tok_major, feat_major, zf_spec],
        out_shape=[jax.ShapeDtypeStruct((bsz, ATT_W, seq), BF16),
                   jax.ShapeDtypeStruct((bsz, seq, ATT_W), BF16),
                   jax.ShapeDtypeStruct((bsz, ATT_W, seq), BF16),
                   jax.ShapeDtypeStruct((bsz, seq, LANES), F32)],
        compiler_params=_params(("arbitrary", "arbitrary")),
        name="qkv_proj",
    )(h, g.reshape(1, d), w_qt, w_k, w_vt, w_f)


def _fgate_kernel(zf_ref, bf_ref, o_ref, *, seq):
    z = zf_ref[...] + bf_ref[...]
    x = -(jnp.maximum(-z, 0.0) + jnp.log1p(jnp.exp(-jnp.abs(z))))
    row = lax.broadcasted_iota(jnp.int32, (seq, LANES), 0)
    sh = 1
    while sh < seq:
        x = x + jnp.where(row >= sh, pltpu.roll(x, sh, 0), 0.0)
        sh *= 2
    x = x * LOG2E
    pieces = []
    for _ in range(F_PIECES):
        piece = x.astype(BF16)
        pieces.append(piece)
        x = x - piece.astype(F32)
    src = lax.broadcasted_iota(jnp.int32, (F_PIECES * LANES, N_PAIR * LANES), 0)
    dst = lax.broadcasted_iota(jnp.int32, (F_PIECES * LANES, N_PAIR * LANES), 1)
    piece_id, head = src // LANES, src % LANES
    pair, slot = dst // LANES, dst % LANES
    hit = (head // 2 == pair) & (slot == F_PIECES * (head % 2) + piece_id) & (head < H_FOX)
    place = jnp.where(hit, 1.0, 0.0).astype(BF16)
    o_ref[...] = jnp.dot(jnp.concatenate(pieces, axis=-1), place,
                         preferred_element_type=F32).astype(BF16)


def _fgate_cumsum(zf, b_f):
    bsz, seq, _ = zf.shape
    return pl.pallas_call(
        functools.partial(_fgate_kernel, seq=seq),
        grid=(bsz,),
        in_specs=[pl.BlockSpec((None, seq, LANES), lambda b: (b, 0, 0)), _const_spec((1, LANES))],
        out_specs=pl.BlockSpec((None, seq, N_PAIR * LANES), lambda b: (b, 0, 0)),
        out_shape=jax.ShapeDtypeStruct((bsz, seq, N_PAIR * LANES), BF16),
        compiler_params=_params(("arbitrary",)),
        name="fgate_cumsum",
    )(zf, jnp.pad(b_f, (0, LANES - H_FOX)).reshape(1, LANES))


ACC_ROWS = HEAD_DIM + BF16_ROWS


def _fill_value_rows(vaug_sc, vt_ref, n_heads):
    ones = jnp.ones((BF16_ROWS, vt_ref.shape[1]), BF16)
    for h in range(n_heads):
        vaug_sc[h, 0:HEAD_DIM, :] = vt_ref[h * HEAD_DIM:(h + 1) * HEAD_DIM, :]
        vaug_sc[h, HEAD_DIM:, :] = ones


def _dual_pipeline(prefix, loop_tile, n_loop, extra, tk, n_heads, vaug_sc, slots, m_sc, acc_sc):
    def stage_q(tile, slot):
        s_sc, t_sc, _, _ = slots[slot]
        for h, s_t in enumerate(tile[0]()):
            s_sc[h] = s_t
            t_sc[h] = jnp.max(s_t, axis=0, keepdims=True)

    def stage_x(slot, state):
        s_sc, t_sc, p_sc, a_sc = slots[slot]
        for h in range(n_heads):
            m_prev = m_sc[state, h]
            m_new = jnp.maximum(m_prev, t_sc[h])
            a_sc[h] = jnp.exp2(m_prev - m_new)
            p_sc[h] = jnp.exp2(s_sc[h] - m_new).astype(BF16)
            m_sc[state, h] = m_new

    def stage_v(k0, slot, state):
        _, _, p_sc, a_sc = slots[slot]
        for h in range(n_heads):
            pv = jnp.dot(vaug_sc[h, :, pl.ds(k0, tk)], p_sc[h], preferred_element_type=F32)
            acc_sc[state, h] = a_sc[h] * acc_sc[state, h] + pv

    assert len(prefix) % 2 == 0
    m_sc[...] = jnp.full_like(m_sc, MASK_VALUE)
    acc_sc[...] = jnp.zeros_like(acc_sc)
    stage_q(prefix[0], 0)
    stage_q(prefix[1], 1)
    stage_x(0, 0)
    for u in range(1, len(prefix) - 1):
        stage_q(prefix[u + 1], (u + 1) % 2)
        stage_v(prefix[u - 1][1], (u - 1) % 2, (u - 1) % 2)
        stage_x(u % 2, u % 2)

    def pending_k0(state, k):
        return jnp.where(k == 0, prefix[len(prefix) - 2 + state][1],
                         loop_tile(state, jnp.maximum(k - 1, 0))[1])

    def pair_body(k, carry):
        stage_q(loop_tile(0, k), 0)
        stage_v(pending_k0(0, k), 0, 0)
        stage_x(1, 1)
        stage_q(loop_tile(1, k), 1)
        stage_v(pending_k0(1, k), 1, 1)
        stage_x(0, 0)
        return carry

    lax.fori_loop(0, n_loop, pair_body, 0)
    stage_q(extra, 0)
    stage_v(pending_k0(0, n_loop), 0, 0)
    stage_x(1, 1)
    stage_v(pending_k0(1, n_loop), 1, 1)
    stage_x(0, 1)
    stage_v(extra[1], 0, 1)


def _pipeline_scratch(n_heads, tile):
    one = [pltpu.VMEM((n_heads, tile, tile), F32), pltpu.VMEM((n_heads, 1, tile), F32),
           pltpu.VMEM((n_heads, tile, tile), BF16), pltpu.VMEM((n_heads, 1, tile), F32)]
    return one + one


def _softmax_finish(acc_sc, n_heads):
    blocks = []
    for pair in range(n_heads // 2):
        halves = []
        for h in (2 * pair, 2 * pair + 1):
            acc = acc_sc[h]
            halves.append(acc[0:HEAD_DIM, :] / acc[HEAD_DIM:HEAD_DIM + 1, :])
        out_t = jnp.concatenate(halves, axis=0)
        blocks.append(jnp.transpose(out_t).astype(BF16))
    return blocks


def _head_rows(qt_ref, h, q0, tq):
    pair, e = divmod(h, 2)
    q = qt_ref[pair * HEAD_PAIR_W:(pair + 1) * HEAD_PAIR_W, pl.ds(q0, tq)]
    row = lax.broadcasted_iota(jnp.int32, q.shape, 0)
    mine = (row >= e * HEAD_DIM) & (row < (e + 1) * HEAD_DIM)
    return jnp.where(mine, q, jnp.zeros_like(q))


def _key_le_query(tile):
    key = lax.broadcasted_iota(jnp.int32, (tile, tile), 0)
    query = lax.broadcasted_iota(jnp.int32, (tile, tile), 1)
    return key <= query


def _fox_kernel(qt_ref, k_ref, f_ref, vt_ref, o_ref,
                kaug_sc, vaug_sc, m_sc, acc_sc, *slot_refs, tile):
    g = pl.program_id(1)

    @pl.when(g == 0)
    def _():
        for pair in range(N_PAIR):
            kaug_sc[pair, :, 0:HEAD_PAIR_W] = k_ref[:, pair * HEAD_PAIR_W:(pair + 1) * HEAD_PAIR_W]
            kaug_sc[pair, :, HEAD_PAIR_W:] = f_ref[:, pair * LANES:(pair + 1) * LANES]
        _fill_value_rows(vaug_sc, vt_ref, H_FOX)

    row = lax.broadcasted_iota(jnp.int32, (LANES, tile), 0)
    q_aug = []
    for state in range(2):
        q_aug.append([])
        for h in range(H_FOX):
            e = h % 2
            minus_one = (row >= F_PIECES * e) & (row < F_PIECES * (e + 1))
            sel = jnp.where(minus_one, -1.0, 0.0).astype(BF16)
            q_aug[state].append(jnp.concatenate([_head_rows(qt_ref, h, state * tile, tile), sel],
                                                axis=0))

    def key_tile(state, j, masked):
        k0 = pl.multiple_of(j * tile, tile)

        def scores():
            tiles = [jnp.dot(kaug_sc[h // 2, pl.ds(k0, tile), :], q_aug[state][h],
                             preferred_element_type=F32) for h in range(H_FOX)]
            if masked:
                allowed = _key_le_query(tile)
                tiles = [jnp.where(allowed, s_t, MASK_VALUE) for s_t in tiles]
            return tiles
        return scores, k0

    _dual_pipeline(prefix=[key_tile(0, 2 * g, True), key_tile(1, 2 * g + 1, True)],
                   loop_tile=lambda state, k: key_tile(state, k, False), n_loop=2 * g,
                   extra=key_tile(1, 2 * g, False), tk=tile, n_heads=H_FOX, vaug_sc=vaug_sc,
                   slots=(slot_refs[:4], slot_refs[4:]), m_sc=m_sc, acc_sc=acc_sc)
    for state in range(2):
        for pair, block in enumerate(_softmax_finish(acc_sc.at[state], H_FOX)):
            o_ref[state * tile:(state + 1) * tile, pair * HEAD_PAIR_W:(pair + 1) * HEAD_PAIR_W] = block


def _fox_attention(qt, k, vt, faug):
    bsz, seq, _ = k.shape
    tile = min(ATT_TILE, seq)
    step = 2 * tile
    assert seq % step == 0 and tile % LANES == 0
    width = H_FOX * HEAD_DIM
    return pl.pallas_call(
        functools.partial(_fox_kernel, tile=tile),
        grid=(bsz, seq // step),
        in_specs=[pl.BlockSpec((None, width, step), lambda b, g: (b, 0, g)),
                  pl.BlockSpec((None, seq, width), lambda b, g: (b, 0, 0)),
                  pl.BlockSpec((None, seq, N_PAIR * LANES), lambda b, g: (b, 0, 0)),
                  pl.BlockSpec((None, width, seq), lambda b, g: (b, 0, 0))],
        out_specs=pl.BlockSpec((None, step, width), lambda b, g: (b, g, 0)),
        out_shape=jax.ShapeDtypeStruct((bsz, seq, width), BF16),
        scratch_shapes=[pltpu.VMEM((N_PAIR, seq, 2 * HEAD_PAIR_W), BF16),
                        pltpu.VMEM((H_FOX, ACC_ROWS, seq), BF16),
                        pltpu.VMEM((2, H_FOX, 1, tile), F32),
                        pltpu.VMEM((2, H_FOX, ACC_ROWS, tile), F32)]
        + _pipeline_scratch(H_FOX, tile),
        compiler_params=_params(("arbitrary", "arbitrary")),
        name="fox_attention",
    )(qt, k, faug, vt)


def _t5_bias_tile(dist, rel_ref, head):
    exact = T5_BUCKETS // 2
    d_f = jnp.maximum(dist, 1).astype(F32)
    log_b = exact + (jnp.log(d_f / exact) / math.log(T5_MAX_DIST / exact)
                     * (T5_BUCKETS - exact)).astype(jnp.int32)
    log_b = jnp.minimum(log_b, T5_BUCKETS - 1)
    bucket = jnp.where(dist < exact, dist, log_b)
    out = jnp.zeros(dist.shape, F32)
    for b in range(T5_BUCKETS):
        out = jnp.where(bucket == b, rel_ref[b, head], out)
    return out


def _moba_kernel(rel_ref, qt_ref, k_ref, vt_ref, h_ref, yc_ref, wo_ref, o_ref,
                 kmean_sc, vaug_sc, bias_sc, pen_sc, m_sc, acc_sc, *slot_refs, seq, blk_rows):
    blk = MOBA_BLOCK
    b = pl.program_id(0)
    g = pl.program_id(1)

    @pl.when((b == 0) & (g == 0))
    def _():
        key = lax.broadcasted_iota(jnp.int32, (blk, blk), 0)
        query = lax.broadcasted_iota(jnp.int32, (blk, blk), 1)
        for h in range(H_MOBA):
            for delta in range(2):
                dist = jnp.maximum(delta * blk + query - key, 0)
                bias_sc[h, delta] = _t5_bias_tile(dist, rel_ref, h) * LOG2E

    @pl.when(g == 0)
    def _():
        mrow = lax.broadcasted_iota(jnp.int32, (blk_rows, seq), 0)
        mcol = lax.broadcasted_iota(jnp.int32, (blk_rows, seq), 1) // blk
        avg = jnp.where(mrow == mcol, 1.0 / blk, 0.0).astype(BF16)
        kmean_sc[...] = jnp.dot(avg, k_ref[...], preferred_element_type=F32)
        _fill_value_rows(vaug_sc, vt_ref, H_MOBA)

    own = (2 * g, 2 * g + 1)
    qh = [[_head_rows(qt_ref, h, state * blk, blk) for h in range(H_MOBA)] for state in range(2)]

    blk_id = lax.broadcasted_iota(jnp.int32, (blk_rows, blk), 0)
    q_far = [[], []]
    for state in range(2):
        for h in range(H_MOBA):
            pair = h // 2
            kmean = kmean_sc[:, pair * HEAD_PAIR_W:(pair + 1) * HEAD_PAIR_W]
            km_hi = kmean.astype(BF16)
            km_lo = (kmean - km_hi.astype(F32)).astype(BF16)
            gate = (jnp.dot(km_hi, qh[state][h], preferred_element_type=F32)
                    + jnp.dot(km_lo, qh[state][h], preferred_element_type=F32))
            gate = jnp.where(blk_id < own[state], gate, -jnp.inf)
            sel = jnp.zeros(gate.shape, jnp.bool_)
            for _ in range(MOBA_TOPK):
                best = jnp.max(gate, axis=0, keepdims=True)
                first_id = jnp.min(jnp.where(gate == best, blk_id, blk_rows), axis=0, keepdims=True)
                pick = (blk_id == first_id) & (best > -jnp.inf)
                sel = sel | pick
                gate = jnp.where(pick, -jnp.inf, gate)
            pen = jnp.where(sel, 0.0, MASK_VALUE)
            pen_sc[state, h] = pen
            far = pen + rel_ref[T5_BUCKETS - 1, h] * LOG2E
            far_hi = far.astype(BF16)
            far_lo = (far - far_hi.astype(F32)).astype(BF16)
            pad = jnp.zeros((HEAD_PAIR_W - 2 * blk_rows, blk), BF16)
            q_far[state].append(jnp.concatenate([qh[state][h], far_hi, far_lo, pad], axis=0))

    def key_rows(j, h):
        pair = h // 2
        return k_ref[pl.ds(pl.multiple_of(j * blk, blk), blk),
                     pair * HEAD_PAIR_W:(pair + 1) * HEAD_PAIR_W]

    def own_block(state):
        j = own[state]

        def scores():
            allowed = _key_le_query(blk)
            return [jnp.where(allowed, jnp.dot(key_rows(j, h), qh[state][h],
                                               preferred_element_type=F32) + bias_sc[h, 0],
                              MASK_VALUE) for h in range(H_MOBA)]
        return scores, pl.multiple_of(j * blk, blk)

    def previous_block(state):
        j = jnp.maximum(own[state] - 1, 0)

        def scores():
            return [jnp.dot(key_rows(j, h), qh[state][h], preferred_element_type=F32)
                    + bias_sc[h, 1] + pen_sc[state, h, pl.ds(j, 1), :] for h in range(H_MOBA)]
        return scores, pl.multiple_of(j * blk, blk)

    def older_block(state, j, present=None):
        def scores():
            lane = lax.broadcasted_iota(jnp.int32, (blk, HEAD_PAIR_W), 1)
            one_hot = jnp.where((lane == j) | (lane == blk_rows + j), 1.0, 0.0).astype(BF16)
            tiles = [jnp.dot(jnp.concatenate([key_rows(j, h), one_hot], axis=1), q_far[state][h],
                             preferred_element_type=F32) for h in range(H_MOBA)]
            if present is not None:
                absent = jnp.where(present, 0.0, MASK_VALUE)
                tiles = [s_t + absent for s_t in tiles]
            return tiles
        return scores, pl.multiple_of(j * blk, blk)

    _dual_pipeline(prefix=[own_block(0), own_block(1), previous_block(0), previous_block(1)],
                   loop_tile=lambda state, k: older_block(state, k),
                   n_loop=jnp.maximum(2 * g - 1, 0),
                   extra=older_block(1, jnp.maximum(2 * g - 1, 0), present=g >= 1),
                   tk=blk, n_heads=H_MOBA, vaug_sc=vaug_sc,
                   slots=(slot_refs[:4], slot_refs[4:]), m_sc=m_sc, acc_sc=acc_sc)

    wc = yc_ref.shape[-1]
    for state in range(2):
        y_d = jnp.concatenate(_softmax_finish(acc_sc.at[state], H_MOBA), axis=-1)
        rows = slice(state * blk, (state + 1) * blk)
        o_ref[rows, :] = (h_ref[rows, :]
                          + jnp.dot(yc_ref[rows, :], wo_ref[0:wc, :], preferred_element_type=F32)
                          + jnp.dot(y_d, wo_ref[wc:, :], preferred_element_type=F32))


def _moba_attention_out(qt, k, vt, rel_bias, h, y_c, w_out):
    bsz, seq, d = h.shape
    blk = MOBA_BLOCK
    step = 2 * blk
    assert seq % step == 0
    assert blk + 1 >= T5_MAX_DIST
    blk_rows = -(-(seq // blk) // BF16_ROWS) * BF16_ROWS
    width = H_MOBA * HEAD_DIM
    off = (H_FOX * HEAD_DIM) // width
    return pl.pallas_call(
        functools.partial(_moba_kernel, seq=seq, blk_rows=blk_rows),
        grid_spec=pltpu.PrefetchScalarGridSpec(
            num_scalar_prefetch=1,
            grid=(bsz, seq // step),
            in_specs=[pl.BlockSpec((None, width, step), lambda b, g, rel: (b, off, g)),
                      pl.BlockSpec((None, seq, width), lambda b, g, rel: (b, 0, off)),
                      pl.BlockSpec((None, width, seq), lambda b, g, rel: (b, off, 0)),
                      pl.BlockSpec((None, step, d), lambda b, g, rel: (b, g, 0)),
                      pl.BlockSpec((None, step, y_c.shape[-1]), lambda b, g, rel: (b, g, 0)),
                      pl.BlockSpec(w_out.shape, lambda b, g, rel: (0, 0),
                                   pipeline_mode=pl.Buffered(1))],
            out_specs=pl.BlockSpec((None, step, d), lambda b, g, rel: (b, g, 0)),
            scratch_shapes=[pltpu.VMEM((blk_rows, width), F32),
                            pltpu.VMEM((H_MOBA, ACC_ROWS, seq), BF16),
                            pltpu.VMEM((H_MOBA, 2, blk, blk), F32),
                            pltpu.VMEM((2, H_MOBA, blk_rows, blk), F32),
                            pltpu.VMEM((2, H_MOBA, 1, blk), F32),
                            pltpu.VMEM((2, H_MOBA, ACC_ROWS, blk), F32)]
            + _pipeline_scratch(H_MOBA, blk)),
        out_shape=jax.ShapeDtypeStruct(h.shape, F32),
        compiler_params=_params(("arbitrary", "arbitrary")),
        name="moba_attention_out",
    )(rel_bias, qt, k, vt, h, y_c, w_out.astype(BF16))


def kernel(x, mix_norm_g, ffn_norm_g, final_norm_g, ev_w_in, ev_conv_w, ev_pool_w, ev_pool_scale,
           ev_w_out, od_w_in, od_b_f, od_w_out, rel_bias, ffn_w_in, ffn_conv_w, ffn_conv_b, ffn_w_out):
    depth = mix_norm_g.shape[0]
    h = x
    for layer in range(depth):
        if layer % 2 == 0:
            e = layer // 2
            h = _even_layer(h, mix_norm_g[layer], ev_w_in[e], ev_conv_w[e], ev_pool_w[e],
                            ev_pool_scale[e], ev_w_out[e])
        else:
            o = layer // 2
            qt, k, vt, zf = _qkv_proj(h, mix_norm_g[layer], od_w_in[o])
            faug = _fgate_cumsum(zf, od_b_f[o])
            y_c = _fox_attention(qt, k, vt, faug)
            h = _moba_attention_out(qt, k, vt, rel_bias, h, y_c, od_w_out[o])
        h = _ffn_layer(h, ffn_norm_g[layer], ffn_w_in[layer], ffn_conv_w[layer], ffn_conv_b[layer],
                       ffn_w_out[layer], final_norm_g, final_norm=(layer == depth - 1))
    return h
```

```python
import functools
import math

import jax
import jax.numpy as jnp
from jax import lax
from jax.experimental import pallas as pl
from jax.experimental.pallas import tpu as pltpu

F32 = jnp.float32
BF16 = jnp.bfloat16

D_MODEL = 1024
HEAD_DIM = 64
CONV_W = 3
A_W = D_MODEL // 2
B_W = D_MODEL - A_W
POOL_WINDOWS = (2, 4, 8, 16)
POOL_G = B_W // len(POOL_WINDOWS)
H_FOX = D_MODEL // (2 * HEAD_DIM)
H_MOBA = D_MODEL // (2 * HEAD_DIM)
ATT_W = (H_FOX + H_MOBA) * HEAD_DIM
ATTN_SCALE = HEAD_DIM ** -0.5
MOBA_BLOCK = 256
MOBA_TOPK = 3
T5_BUCKETS = 32
T5_MAX_DIST = 128
D_FF = 2816
RMS_EPS = 1e-6

LANES = 128
SUBLANES = 8
BF16_ROWS = 16
VMEM_LIMIT_BYTES = 56 * 1024 * 1024

HEAD_PAIR_W = 2 * HEAD_DIM
N_PAIR = H_FOX // 2
MASK_VALUE = -1e30
LOG2E = math.log2(math.e)
F_PIECES = 3

TOKEN_TILE = 512
FFN_CHUNK = D_FF
ATT_TILE = MOBA_BLOCK
POOL_HALO = 32


def _rmsnorm(x, g):
    ms = jnp.mean(x * x, axis=-1, keepdims=True)
    return x * lax.rsqrt(ms + RMS_EPS) * g


def _const_spec(shape):
    nd = len(shape)
    return pl.BlockSpec(shape, lambda *_: (0,) * nd, pipeline_mode=pl.Buffered(1))


def _layer_spec(stacked_shape, index):
    nd = len(stacked_shape)
    return pl.BlockSpec((None,) + tuple(stacked_shape[1:]), lambda *_: (index,) + (0,) * (nd - 1),
                        pipeline_mode=pl.Buffered(1))


def _params(semantics):
    return pltpu.CompilerParams(dimension_semantics=semantics, vmem_limit_bytes=VMEM_LIMIT_BYTES)


def _even_kernel(h_ref, g_ref, win_ref, cw_ref, pw_ref, ps_ref, wout_ref, o_ref,
                 cv_sc, u_sc, lvl_sc, tailcv_sc, tailu_sc, *, tile):
    s = pl.program_id(1)
    halo = POOL_HALO

    @pl.when(s == 0)
    def _():
        tailcv_sc[...] = jnp.zeros_like(tailcv_sc)
        tailu_sc[...] = jnp.zeros_like(tailu_sc)

    h = h_ref[...]
    hn = _rmsnorm(h, g_ref[...]).astype(BF16)
    z = jnp.dot(hn, win_ref[...], preferred_element_type=F32)
    gate_b = z[:, :A_W]
    cv = z[:, A_W:2 * A_W] * z[:, 2 * A_W:3 * A_W]
    pool_in = z[:, 3 * A_W:]

    cv_sc[0:SUBLANES, :] = tailcv_sc[...]
    cv_sc[SUBLANES:SUBLANES + tile, :] = cv
    tailcv_sc[...] = cv[tile - SUBLANES:, :]
    conv = (cw_ref[0:1, :] * cv_sc[SUBLANES - 2:SUBLANES - 2 + tile, :]
            + cw_ref[1:2, :] * cv_sc[SUBLANES - 1:SUBLANES - 1 + tile, :]
            + cw_ref[2:3, :] * cv)
    y_a = gate_b * conv

    u_sc[0:halo, :] = tailu_sc[...]
    u_sc[halo:halo + tile, :] = pool_in
    tailu_sc[...] = pool_in[tile - halo:, :]
    pos = s * tile + lax.broadcasted_iota(jnp.int32, (tile, 1), 0)
    ys = [y_a.astype(BF16)]
    for g, w in enumerate(POOL_WINDOWS):
        cols = slice(g * POOL_G, (g + 1) * POOL_G)
        n_lvl = g + 1
        src, src_cols = u_sc, cols
        for lvl in range(1, n_lvl):
            sh = 2 ** (lvl - 1)
            lo = SUBLANES * lvl
            dst = lvl_sc.at[lvl % 2]
            dst[lo:halo + tile, :] = (src[lo:halo + tile, src_cols]
                                      + src[lo - sh:halo + tile - sh, src_cols])
            src, src_cols = dst, slice(None)
        sh = 2 ** (n_lvl - 1)
        wsum = src[halo:halo + tile, src_cols] + src[halo - sh:halo + tile - sh, src_cols]
        cnt = jnp.minimum(pos + 1, w).astype(F32)
        p = wsum / cnt - pool_in[:, cols]
        yb = jnp.dot(p.astype(BF16), pw_ref[g], preferred_element_type=F32) * ps_ref[:, cols]
        ys.append(yb.astype(BF16))
    y = jnp.concatenate(ys, axis=-1)
    o_ref[...] = h + jnp.dot(y, wout_ref[...], preferred_element_type=F32)


def _even_layer(h, layer, e, g, w_in, conv_w, pool_w, pool_scale, w_out):
    bsz, seq, d = h.shape
    tile = min(2 * TOKEN_TILE, seq)
    assert seq % tile == 0 and tile >= POOL_HALO
    act_spec = pl.BlockSpec((None, tile, d), lambda b, s: (b, s, 0))
    return pl.pallas_call(
        functools.partial(_even_kernel, tile=tile),
        grid=(bsz, seq // tile),
        in_specs=[act_spec, _layer_spec(g.shape, layer), _layer_spec(w_in.shape, e),
                  _layer_spec(conv_w.shape, e), _layer_spec(pool_w.shape, e),
                  _layer_spec(pool_scale.shape, e), _layer_spec(w_out.shape, e)],
        out_specs=act_spec,
        out_shape=jax.ShapeDtypeStruct(h.shape, F32),
        scratch_shapes=[pltpu.VMEM((SUBLANES + tile, A_W), F32),
                        pltpu.VMEM((POOL_HALO + tile, B_W), F32),
                        pltpu.VMEM((2, POOL_HALO + tile, POOL_G), F32),
                        pltpu.VMEM((SUBLANES, A_W), F32),
                        pltpu.VMEM((POOL_HALO, B_W), F32)],
        compiler_params=_params(("arbitrary", "arbitrary")),
        name="even_mixer",
    )(h, g, w_in, conv_w, pool_w, pool_scale, w_out)


def _ffn_kernel(h_ref, g_ref, win_ref, cw_ref, cb_ref, wout_ref, fg_ref, o_ref, ext_sc, tail_sc,
                *, tile, final_norm):
    s = pl.program_id(1)

    @pl.when(s == 0)
    def _():
        tail_sc[...] = jnp.zeros_like(tail_sc)

    h = h_ref[...]
    hn = _rmsnorm(h, g_ref[...]).astype(BF16)
    acc = h
    for c in range(D_FF // FFN_CHUNK):
        cols = slice(c * FFN_CHUNK, (c + 1) * FFN_CHUNK)
        gcols = slice(D_FF + c * FFN_CHUNK, D_FF + (c + 1) * FFN_CHUNK)
        u = jnp.dot(hn, win_ref[:, cols], preferred_element_type=F32)
        gt = jnp.dot(hn, win_ref[:, gcols], preferred_element_type=F32)
        ext_sc[0:SUBLANES, :] = tail_sc[:, cols]
        ext_sc[SUBLANES:SUBLANES + tile, :] = u
        tail_sc[:, cols] = u[tile - SUBLANES:, :]
        a = (cw_ref[0:1, cols] * ext_sc[SUBLANES - 2:SUBLANES - 2 + tile, :]
             + cw_ref[1:2, cols] * ext_sc[SUBLANES - 1:SUBLANES - 1 + tile, :]
             + cw_ref[2:3, cols] * u
             + cb_ref[:, cols])
        act = a * (1.0 / (1.0 + jnp.exp(-a))) * gt
        acc = acc + jnp.dot(act.astype(BF16), wout_ref[cols, :], preferred_element_type=F32)
    if final_norm:
        acc = _rmsnorm(acc, fg_ref[...])
    o_ref[...] = acc


def _ffn_layer(h, layer, g, w_in, conv_w, conv_b, w_out, final_g, final_norm):
    bsz, seq, d = h.shape
    tile = min(TOKEN_TILE, seq)
    assert seq % tile == 0
    act_spec = pl.BlockSpec((None, tile, d), lambda b, s: (b, s, 0))
    return pl.pallas_call(
        functools.partial(_ffn_kernel, tile=tile, final_norm=final_norm),
        grid=(bsz, seq // tile),
        in_specs=[act_spec, _layer_spec(g.shape, layer), _layer_spec(w_in.shape, layer),
                  _layer_spec(conv_w.shape, layer), _layer_spec(conv_b.shape, layer),
                  _layer_spec(w_out.shape, layer), _const_spec((1, d))],
        out_specs=act_spec,
        out_shape=jax.ShapeDtypeStruct(h.shape, F32),
        scratch_shapes=[pltpu.VMEM((SUBLANES + tile, FFN_CHUNK), F32),
                        pltpu.VMEM((SUBLANES, D_FF), F32)],
        compiler_params=_params(("arbitrary", "arbitrary")),
        name="conv_ffn",
    )(h, g, w_in, conv_w, conv_b, w_out, final_g.reshape(1, d))


_NT_DIMS = (((1,), (1,)), ((), ()))


def _qkv_kernel(h_ref, g_ref, wqt_ref, wk_ref, wvt_ref, wf_ref, qt_ref, k_ref, vt_ref, zf_ref):
    hn = _rmsnorm(h_ref[...], g_ref[...]).astype(BF16)
    qt = lax.dot_general(wqt_ref[...], hn, _NT_DIMS, preferred_element_type=F32)
    qt_ref[...] = (qt * (ATTN_SCALE * LOG2E)).astype(BF16)
    k_ref[...] = jnp.dot(hn, wk_ref[...], preferred_element_type=F32).astype(BF16)
    vt_ref[...] = lax.dot_general(wvt_ref[...], hn, _NT_DIMS, preferred_element_type=F32).astype(BF16)
    zf_ref[...] = jnp.dot(hn, wf_ref[...], preferred_element_type=F32)


def _qkv_proj(h, g, w_in):
    bsz, seq, d = h.shape
    tile = min(2 * TOKEN_TILE, seq)
    assert seq % tile == 0
    w_qt = w_in[:, :ATT_W].T.astype(BF16)
    w_k = w_in[:, ATT_W:2 * ATT_W].astype(BF16)
    w_vt = w_in[:, 2 * ATT_W:3 * ATT_W].T.astype(BF16)
    w_f = jnp.pad(w_in[:, 3 * ATT_W:], ((0, 0), (0, LANES - H_FOX))).astype(BF16)
    act_spec = pl.BlockSpec((None, tile, d), lambda b, s: (b, s, 0))
    tok_major = pl.BlockSpec((None, tile, ATT_W), lambda b, s: (b, s, 0))
    feat_major = pl.BlockSpec((None, ATT_W, tile), lambda b, s: (b, 0, s))
    zf_spec = pl.BlockSpec((None, tile, LANES), lambda b, s: (b, s, 0))
    return pl.pallas_call(
        _qkv_kernel,
        grid=(bsz, seq // tile),
        in_specs=[act_spec, _const_spec((1, d)), _const_spec(w_qt.shape), _const_spec(w_k.shape),
                  _const_spec(w_vt.shape), _const_spec(w_f.shape)],
        out_specs=[feat_major, tok_major, feat_major, zf_spec],
        out_shape=[jax.ShapeDtypeStruct((bsz, ATT_W, seq), BF16),
                   jax.ShapeDtypeStruct((bsz, seq, ATT_W), BF16),
                   jax.ShapeDtypeStruct((bsz, ATT_W, seq), BF16),
                   jax.ShapeDtypeStruct((bsz, seq, LANES), F32)],
        compiler_params=_params(("arbitrary", "arbitrary")),
        name="qkv_proj",
    )(h, g.reshape(1, d), w_qt, w_k, w_vt, w_f)


def _fgate_kernel(zf_ref, bf_ref, o_ref, *, seq):
    z = zf_ref[...] + bf_ref[...]
    x = -(jnp.maximum(-z, 0.0) + jnp.log1p(jnp.exp(-jnp.abs(z))))
    row = lax.broadcasted_iota(jnp.int32, (seq, LANES), 0)
    sh = 1
    while sh < seq:
        x = x + jnp.where(row >= sh, pltpu.roll(x, sh, 0), 0.0)
        sh *= 2
    x = x * LOG2E
    pieces = []
    for _ in range(F_PIECES):
        piece = x.astype(BF16)
        pieces.append(piece)
        x = x - piece.astype(F32)
    src = lax.broadcasted_iota(jnp.int32, (F_PIECES * LANES, N_PAIR * LANES), 0)
    dst = lax.broadcasted_iota(jnp.int32, (F_PIECES * LANES, N_PAIR * LANES), 1)
    piece_id, head = src // LANES, src % LANES
    pair, slot = dst // LANES, dst % LANES
    hit = (head // 2 == pair) & (slot == F_PIECES * (head % 2) + piece_id) & (head < H_FOX)
    place = jnp.where(hit, 1.0, 0.0).astype(BF16)
    o_ref[...] = jnp.dot(jnp.concatenate(pieces, axis=-1), place,
                         preferred_element_type=F32).astype(BF16)


def _fgate_cumsum(zf, b_f):
    bsz, seq, _ = zf.shape
    return pl.pallas_call(
        functools.partial(_fgate_kernel, seq=seq),
        grid=(bsz,),
        in_specs=[pl.BlockSpec((None, seq, LANES), lambda b: (b, 0, 0)), _const_spec((1, LANES))],
        out_specs=pl.BlockSpec((None, seq, N_PAIR * LANES), lambda b: (b, 0, 0)),
        out_shape=jax.ShapeDtypeStruct((bsz, seq, N_PAIR * LANES), BF16),
        compiler_params=_params(("arbitrary",)),
        name="fgate_cumsum",
    )(zf, jnp.pad(b_f, (0, LANES - H_FOX)).reshape(1, LANES))


ACC_ROWS = HEAD_DIM + BF16_ROWS


def _fill_value_rows(vaug_sc, vt_ref, n_heads):
    ones = jnp.ones((BF16_ROWS, vt_ref.shape[1]), BF16)
    for h in range(n_heads):
        vaug_sc[h, 0:HEAD_DIM, :] = vt_ref[h * HEAD_DIM:(h + 1) * HEAD_DIM, :]
        vaug_sc[h, HEAD_DIM:, :] = ones


def _dual_pipeline(prefix, loop_tile, n_loop, extra, tk, n_heads, vaug_sc, slots, m_sc, acc_sc):
    def stage_q(tile, slot):
        s_sc, t_sc, _, _ = slots[slot]
        for h, s_t in enumerate(tile[0]()):
            s_sc[h] = s_t
            t_sc[h] = jnp.max(s_t, axis=0, keepdims=True)

    def stage_x(slot, state):
        s_sc, t_sc, p_sc, a_sc = slots[slot]
        for h in range(n_heads):
            m_prev = m_sc[state, h]
            m_new = jnp.maximum(m_prev, t_sc[h])
            a_sc[h] = jnp.exp2(m_prev - m_new)
            p_sc[h] = jnp.exp2(s_sc[h] - m_new).astype(BF16)
            m_sc[state, h] = m_new

    def stage_v(k0, slot, state):
        _, _, p_sc, a_sc = slots[slot]
        for h in range(n_heads):
            pv = jnp.dot(vaug_sc[h, :, pl.ds(k0, tk)], p_sc[h], preferred_element_type=F32)
            acc_sc[state, h] = a_sc[h] * acc_sc[state, h] + pv

    assert len(prefix) % 2 == 0
    m_sc[...] = jnp.full_like(m_sc, MASK_VALUE)
    acc_sc[...] = jnp.zeros_like(acc_sc)
    stage_q(prefix[0], 0)
    stage_q(prefix[1], 1)
    stage_x(0, 0)
    for u in range(1, len(prefix) - 1):
        stage_q(prefix[u + 1], (u + 1) % 2)
        stage_v(prefix[u - 1][1], (u - 1) % 2, (u - 1) % 2)
        stage_x(u % 2, u % 2)

    def pending_k0(state, k):
        return jnp.where(k == 0, prefix[len(prefix) - 2 + state][1],
                         loop_tile(state, jnp.maximum(k - 1, 0))[1])

    def pair_body(k, carry):
        stage_q(loop_tile(0, k), 0)
        stage_v(pending_k0(0, k), 0, 0)
        stage_x(1, 1)
        stage_q(loop_tile(1, k), 1)
        stage_v(pending_k0(1, k), 1, 1)
        stage_x(0, 0)
        return carry

    lax.fori_loop(0, n_loop, pair_body, 0)
    stage_q(extra, 0)
    stage_v(pending_k0(0, n_loop), 0, 0)
    stage_x(1, 1)
    stage_v(pending_k0(1, n_loop), 1, 1)
    stage_x(0, 1)
    stage_v(extra[1], 0, 1)


def _pipeline_scratch(n_heads, tile):
    one = [pltpu.VMEM((n_heads, tile, tile), F32), pltpu.VMEM((n_heads, 1, tile), F32),
           pltpu.VMEM((n_heads, tile, tile), BF16), pltpu.VMEM((n_heads, 1, tile), F32)]
    return one + one


def _softmax_finish(acc_sc, n_heads):
    blocks = []
    for pair in range(n_heads // 2):
        halves = []
        for h in (2 * pair, 2 * pair + 1):
            acc = acc_sc[h]
            halves.append(acc[0:HEAD_DIM, :] / acc[HEAD_DIM:HEAD_DIM + 1, :])
        out_t = jnp.concatenate(halves, axis=0)
        blocks.append(jnp.transpose(out_t).astype(BF16))
    return blocks


def _head_rows(qt_ref, h, q0, tq):
    pair, e = divmod(h, 2)
    q = qt_ref[pair * HEAD_PAIR_W:(pair + 1) * HEAD_PAIR_W, pl.ds(q0, tq)]
    row = lax.broadcasted_iota(jnp.int32, q.shape, 0)
    mine = (row >= e * HEAD_DIM) & (row < (e + 1) * HEAD_DIM)
    return jnp.where(mine, q, jnp.zeros_like(q))


def _key_le_query(tile):
    key = lax.broadcasted_iota(jnp.int32, (tile, tile), 0)
    query = lax.broadcasted_iota(jnp.int32, (tile, tile), 1)
    return key <= query


def _fox_kernel(qt_ref, k_ref, f_ref, vt_ref, o_ref,
                kaug_sc, vaug_sc, m_sc, acc_sc, *slot_refs, tile):
    g = pl.program_id(1)

    @pl.when(g == 0)
    def _():
        for pair in range(N_PAIR):
            kaug_sc[pair, :, 0:HEAD_PAIR_W] = k_ref[:, pair * HEAD_PAIR_W:(pair + 1) * HEAD_PAIR_W]
            kaug_sc[pair, :, HEAD_PAIR_W:] = f_ref[:, pair * LANES:(pair + 1) * LANES]
        _fill_value_rows(vaug_sc, vt_ref, H_FOX)

    row = lax.broadcasted_iota(jnp.int32, (LANES, tile), 0)
    q_aug = []
    for state in range(2):
        q_aug.append([])
        for h in range(H_FOX):
            e = h % 2
            minus_one = (row >= F_PIECES * e) & (row < F_PIECES * (e + 1))
            sel = jnp.where(minus_one, -1.0, 0.0).astype(BF16)
            q_aug[state].append(jnp.concatenate([_head_rows(qt_ref, h, state * tile, tile), sel],
                                                axis=0))

    def key_tile(state, j, masked):
        k0 = pl.multiple_of(j * tile, tile)

        def scores():
            tiles = [jnp.dot(kaug_sc[h // 2, pl.ds(k0, tile), :], q_aug[state][h],
                             preferred_element_type=F32) for h in range(H_FOX)]
            if masked:
                allowed = _key_le_query(tile)
                tiles = [jnp.where(allowed, s_t, MASK_VALUE) for s_t in tiles]
            return tiles
        return scores, k0

    _dual_pipeline(prefix=[key_tile(0, 2 * g, True), key_tile(1, 2 * g + 1, True)],
                   loop_tile=lambda state, k: key_tile(state, k, False), n_loop=2 * g,
                   extra=key_tile(1, 2 * g, False), tk=tile, n_heads=H_FOX, vaug_sc=vaug_sc,
                   slots=(slot_refs[:4], slot_refs[4:]), m_sc=m_sc, acc_sc=acc_sc)
    for state in range(2):
        for pair, block in enumerate(_softmax_finish(acc_sc.at[state], H_FOX)):
            o_ref[state * tile:(state + 1) * tile, pair * HEAD_PAIR_W:(pair + 1) * HEAD_PAIR_W] = block


def _fox_attention(qt, k, vt, faug):
    bsz, seq, _ = k.shape
    tile = min(ATT_TILE, seq)
    step = 2 * tile
    assert seq % step == 0 and tile % LANES == 0
    width = H_FOX * HEAD_DIM
    return pl.pallas_call(
        functools.partial(_fox_kernel, tile=tile),
        grid=(bsz, seq // step),
        in_specs=[pl.BlockSpec((None, width, step), lambda b, g: (b, 0, g)),
                  pl.BlockSpec((None, seq, width), lambda b, g: (b, 0, 0)),
                  pl.BlockSpec((None, seq, N_PAIR * LANES), lambda b, g: (b, 0, 0)),
                  pl.BlockSpec((None, width, seq), lambda b, g: (b, 0, 0))],
        out_specs=pl.BlockSpec((None, step, width), lambda b, g: (b, g, 0)),
        out_shape=jax.ShapeDtypeStruct((bsz, seq, width), BF16),
        scratch_shapes=[pltpu.VMEM((N_PAIR, seq, 2 * HEAD_PAIR_W), BF16),
                        pltpu.VMEM((H_FOX, ACC_ROWS, seq), BF16),
                        pltpu.VMEM((2, H_FOX, 1, tile), F32),
                        pltpu.VMEM((2, H_FOX, ACC_ROWS, tile), F32)]
        + _pipeline_scratch(H_FOX, tile),
        compiler_params=_params(("arbitrary", "arbitrary")),
        name="fox_attention",
    )(qt, k, faug, vt)


def _t5_bias_tile(dist, rel_ref, head):
    exact = T5_BUCKETS // 2
    d_f = jnp.maximum(dist, 1).astype(F32)
    log_b = exact + (jnp.log(d_f / exact) / math.log(T5_MAX_DIST / exact)
                     * (T5_BUCKETS - exact)).astype(jnp.int32)
    log_b = jnp.minimum(log_b, T5_BUCKETS - 1)
    bucket = jnp.where(dist < exact, dist, log_b)
    out = jnp.zeros(dist.shape, F32)
    for b in range(T5_BUCKETS):
        out = jnp.where(bucket == b, rel_ref[b, head], out)
    return out


def _moba_kernel(rel_ref, qt_ref, k_ref, vt_ref, h_ref, yc_ref, wo_ref, o_ref,
                 kmean_sc, vaug_sc, bias_sc, pen_sc, m_sc, acc_sc, *slot_refs, seq, blk_rows):
    blk = MOBA_BLOCK
    b = pl.program_id(0)
    g = pl.program_id(1)

    @pl.when((b == 0) & (g == 0))
    def _():
        key = lax.broadcasted_iota(jnp.int32, (blk, blk), 0)
        query = lax.broadcasted_iota(jnp.int32, (blk, blk), 1)
        for h in range(H_MOBA):
            for delta in range(2):
                dist = jnp.maximum(delta * blk + query - key, 0)
                bias_sc[h, delta] = _t5_bias_tile(dist, rel_ref, h) * LOG2E

    @pl.when(g == 0)
    def _():
        mrow = lax.broadcasted_iota(jnp.int32, (blk_rows, seq), 0)
        mcol = lax.broadcasted_iota(jnp.int32, (blk_rows, seq), 1) // blk
        avg = jnp.where(mrow == mcol, 1.0 / blk, 0.0).astype(BF16)
        kmean_sc[...] = jnp.dot(avg, k_ref[...], preferred_element_type=F32)
        _fill_value_rows(vaug_sc, vt_ref, H_MOBA)

    own = (2 * g, 2 * g + 1)
    qh = [[_head_rows(qt_ref, h, state * blk, blk) for h in range(H_MOBA)] for state in range(2)]

    blk_id = lax.broadcasted_iota(jnp.int32, (blk_rows, blk), 0)
    q_far = [[], []]
    for state in range(2):
        for h in range(H_MOBA):
            pair = h // 2
            kmean = kmean_sc[:, pair * HEAD_PAIR_W:(pair + 1) * HEAD_PAIR_W]
            km_hi = kmean.astype(BF16)
            km_lo = (kmean - km_hi.astype(F32)).astype(BF16)
            gate = (jnp.dot(km_hi, qh[state][h], preferred_element_type=F32)
                    + jnp.dot(km_lo, qh[state][h], preferred_element_type=F32))
            gate = jnp.where(blk_id < own[state], gate, -jnp.inf)
            sel = jnp.zeros(gate.shape, jnp.bool_)
            for _ in range(MOBA_TOPK):
                best = jnp.max(gate, axis=0, keepdims=True)
                first_id = jnp.min(jnp.where(gate == best, blk_id, blk_rows), axis=0, keepdims=True)
                pick = (blk_id == first_id) & (best > -jnp.inf)
                sel = sel | pick
                gate = jnp.where(pick, -jnp.inf, gate)
            pen = jnp.where(sel, 0.0, MASK_VALUE)
            pen_sc[state, h] = pen
            far = pen + rel_ref[T5_BUCKETS - 1, h] * LOG2E
            far_hi = far.astype(BF16)
            far_lo = (far - far_hi.astype(F32)).astype(BF16)
            pad = jnp.zeros((HEAD_PAIR_W - 2 * blk_rows, blk), BF16)
            q_far[state].append(jnp.concatenate([qh[state][h], far_hi, far_lo, pad], axis=0))

    def key_rows(j, h):
        pair = h // 2
        return k_ref[pl.ds(pl.multiple_of(j * blk, blk), blk),
                     pair * HEAD_PAIR_W:(pair + 1) * HEAD_PAIR_W]

    def own_block(state):
        j = own[state]

        def scores():
            allowed = _key_le_query(blk)
            return [jnp.where(allowed, jnp.dot(key_rows(j, h), qh[state][h],
                                               preferred_element_type=F32) + bias_sc[h, 0],
                              MASK_VALUE) for h in range(H_MOBA)]
        return scores, pl.multiple_of(j * blk, blk)

    def previous_block(state):
        j = jnp.maximum(own[state] - 1, 0)

        def scores():
            return [jnp.dot(key_rows(j, h), qh[state][h], preferred_element_type=F32)
                    + bias_sc[h, 1] + pen_sc[state, h, pl.ds(j, 1), :] for h in range(H_MOBA)]
        return scores, pl.multiple_of(j * blk, blk)

    def older_block(state, j, present=None):
        def scores():
            lane = lax.broadcasted_iota(jnp.int32, (blk, HEAD_PAIR_W), 1)
            one_hot = jnp.where((lane == j) | (lane == blk_rows + j), 1.0, 0.0).astype(BF16)
            tiles = [jnp.dot(jnp.concatenate([key_rows(j, h), one_hot], axis=1), q_far[state][h],
                             preferred_element_type=F32) for h in range(H_MOBA)]
            if present is not None:
                absent = jnp.where(present, 0.0, MASK_VALUE)
                tiles = [s_t + absent for s_t in tiles]
            return tiles
        return scores, pl.multiple_of(j * blk, blk)

    _dual_pipeline(prefix=[own_block(0), own_block(1), previous_block(0), previous_block(1)],
                   loop_tile=lambda state, k: older_block(state, k),
                   n_loop=jnp.maximum(2 * g - 1, 0),
                   extra=older_block(1, jnp.maximum(2 * g - 1, 0), present=g >= 1),
                   tk=blk, n_heads=H_MOBA, vaug_sc=vaug_sc,
                   slots=(slot_refs[:4], slot_refs[4:]), m_sc=m_sc, acc_sc=acc_sc)

    wc = yc_ref.shape[-1]
    for state in range(2):
        y_d = jnp.concatenate(_softmax_finish(acc_sc.at[state], H_MOBA), axis=-1)
        rows = slice(state * blk, (state + 1) * blk)
        o_ref[rows, :] = (h_ref[rows, :]
                          + jnp.dot(yc_ref[rows, :], wo_ref[0:wc, :], preferred_element_type=F32)
                          + jnp.dot(y_d, wo_ref[wc:, :], preferred_element_type=F32))


def _moba_attention_out(qt, k, vt, rel_bias, h, y_c, w_out, o):
    bsz, seq, d = h.shape
    blk = MOBA_BLOCK
    step = 2 * blk
    assert seq % step == 0
    assert blk + 1 >= T5_MAX_DIST
    blk_rows = -(-(seq // blk) // BF16_ROWS) * BF16_ROWS
    width = H_MOBA * HEAD_DIM
    off = (H_FOX * HEAD_DIM) // width
    return pl.pallas_call(
        functools.partial(_moba_kernel, seq=seq, blk_rows=blk_rows),
        grid_spec=pltpu.PrefetchScalarGridSpec(
            num_scalar_prefetch=1,
            grid=(bsz, seq // step),
            in_specs=[pl.BlockSpec((None, width, step), lambda b, g, rel: (b, off, g)),
                      pl.BlockSpec((None, seq, width), lambda b, g, rel: (b, 0, off)),
                      pl.BlockSpec((None, width, seq), lambda b, g, rel: (b, off, 0)),
                      pl.BlockSpec((None, step, d), lambda b, g, rel: (b, g, 0)),
                      pl.BlockSpec((None, step, y_c.shape[-1]), lambda b, g, rel: (b, g, 0)),
                      pl.BlockSpec((None,) + w_out.shape[1:], lambda b, g, rel: (o, 0, 0),
                                   pipeline_mode=pl.Buffered(1))],
            out_specs=pl.BlockSpec((None, step, d), lambda b, g, rel: (b, g, 0)),
            scratch_shapes=[pltpu.VMEM((blk_rows, width), F32),
                            pltpu.VMEM((H_MOBA, ACC_ROWS, seq), BF16),
                            pltpu.VMEM((H_MOBA, 2, blk, blk), F32),
                            pltpu.VMEM((2, H_MOBA, blk_rows, blk), F32),
                            pltpu.VMEM((2, H_MOBA, 1, blk), F32),
                            pltpu.VMEM((2, H_MOBA, ACC_ROWS, blk), F32)]
            + _pipeline_scratch(H_MOBA, blk)),
        out_shape=jax.ShapeDtypeStruct(h.shape, F32),
        compiler_params=_params(("arbitrary", "arbitrary")),
        name="moba_attention_out",
    )(rel_bias, qt, k, vt, h, y_c, w_out)


def kernel(x, mix_norm_g, ffn_norm_g, final_norm_g, ev_w_in, ev_conv_w, ev_pool_w, ev_pool_scale,
           ev_w_out, od_w_in, od_b_f, od_w_out, rel_bias, ffn_w_in, ffn_conv_w, ffn_conv_b, ffn_w_out):
    depth, d = mix_norm_g.shape
    mix_g = mix_norm_g.reshape(depth, 1, d)
    ffn_g = ffn_norm_g.reshape(depth, 1, d)
    ev_w_in_b, ev_pool_w_b, ev_w_out_b = (w.astype(BF16) for w in (ev_w_in, ev_pool_w, ev_w_out))
    ev_scale = ev_pool_scale.reshape(ev_pool_scale.shape[0], 1, B_W)
    od_w_out_b = od_w_out.astype(BF16)
    ffn_w_in_b, ffn_w_out_b = ffn_w_in.astype(BF16), ffn_w_out.astype(BF16)
    ffn_b = ffn_conv_b.reshape(depth, 1, D_FF)
    h = x
    for layer in range(depth):
        if layer % 2 == 0:
            h = _even_layer(h, layer, layer // 2, mix_g, ev_w_in_b, ev_conv_w, ev_pool_w_b, ev_scale,
                            ev_w_out_b)
        else:
            o = layer // 2
            qt, k, vt, zf = _qkv_proj(h, mix_norm_g[layer], od_w_in[o])
            faug = _fgate_cumsum(zf, od_b_f[o])
            y_c = _fox_attention(qt, k, vt, faug)
            h = _moba_attention_out(qt, k, vt, rel_bias, h, y_c, od_w_out_b, o)
        h = _ffn_layer(h, layer, ffn_g, ffn_w_in_b, ffn_conv_w, ffn_b, ffn_w_out_b, final_norm_g,
                       final_norm=(layer == depth - 1))
    return h
```

```python
import functools
import math

import jax
import jax.numpy as jnp
from jax import lax
from jax.experimental import pallas as pl
from jax.experimental.pallas import tpu as pltpu

F32 = jnp.float32
BF16 = jnp.bfloat16

D_MODEL = 1024
HEAD_DIM = 64
CONV_W = 3
A_W = D_MODEL // 2
B_W = D_MODEL - A_W
POOL_WINDOWS = (2, 4, 8, 16)
POOL_G = B_W // len(POOL_WINDOWS)
H_FOX = D_MODEL // (2 * HEAD_DIM)
H_MOBA = D_MODEL // (2 * HEAD_DIM)
ATT_W = (H_FOX + H_MOBA) * HEAD_DIM
ATTN_SCALE = HEAD_DIM ** -0.5
MOBA_BLOCK = 256
MOBA_TOPK = 3
T5_BUCKETS = 32
T5_MAX_DIST = 128
D_FF = 2816
RMS_EPS = 1e-6

LANES = 128
SUBLANES = 8
BF16_ROWS = 16
VMEM_LIMIT_BYTES = 56 * 1024 * 1024

HEAD_PAIR_W = 2 * HEAD_DIM
N_PAIR = H_FOX // 2
MASK_VALUE = -1e30
LOG2E = math.log2(math.e)
F_PIECES = 3

TOKEN_TILE = 512
FFN_CHUNK = D_FF
ATT_TILE = MOBA_BLOCK
POOL_HALO = 32


def _rmsnorm(x, g):
    ms = jnp.mean(x * x, axis=-1, keepdims=True)
    return x * lax.rsqrt(ms + RMS_EPS) * g


def _const_spec(shape):
    nd = len(shape)
    return pl.BlockSpec(shape, lambda *_: (0,) * nd, pipeline_mode=pl.Buffered(1))


def _layer_spec(stacked_shape, index):
    nd = len(stacked_shape)
    return pl.BlockSpec((None,) + tuple(stacked_shape[1:]), lambda *_: (index,) + (0,) * (nd - 1),
                        pipeline_mode=pl.Buffered(1))


def _params(semantics):
    return pltpu.CompilerParams(dimension_semantics=semantics, vmem_limit_bytes=VMEM_LIMIT_BYTES)


def _even_kernel(h_ref, g_ref, win_ref, cw_ref, pw_ref, ps_ref, wout_ref, o_ref,
                 cv_sc, u_sc, lvl_sc, tailcv_sc, tailu_sc, *, tile):
    s = pl.program_id(1)
    halo = POOL_HALO

    @pl.when(s == 0)
    def _():
        tailcv_sc[...] = jnp.zeros_like(tailcv_sc)
        tailu_sc[...] = jnp.zeros_like(tailu_sc)

    h = h_ref[...]
    hn = _rmsnorm(h, g_ref[...]).astype(BF16)
    z = jnp.dot(hn, win_ref[...], preferred_element_type=F32)
    gate_b = z[:, :A_W]
    cv = z[:, A_W:2 * A_W] * z[:, 2 * A_W:3 * A_W]
    pool_in = z[:, 3 * A_W:]

    cv_sc[0:SUBLANES, :] = tailcv_sc[...]
    cv_sc[SUBLANES:SUBLANES + tile, :] = cv
    tailcv_sc[...] = cv[tile - SUBLANES:, :]
    conv = (cw_ref[0:1, :] * cv_sc[SUBLANES - 2:SUBLANES - 2 + tile, :]
            + cw_ref[1:2, :] * cv_sc[SUBLANES - 1:SUBLANES - 1 + tile, :]
            + cw_ref[2:3, :] * cv)
    y_a = gate_b * conv

    u_sc[0:halo, :] = tailu_sc[...]
    u_sc[halo:halo + tile, :] = pool_in
    tailu_sc[...] = pool_in[tile - halo:, :]
    pos = s * tile + lax.broadcasted_iota(jnp.int32, (tile, 1), 0)
    ys = [y_a.astype(BF16)]
    for g, w in enumerate(POOL_WINDOWS):
        cols = slice(g * POOL_G, (g + 1) * POOL_G)
        n_lvl = g + 1
        src, src_cols = u_sc, cols
        for lvl in range(1, n_lvl):
            sh = 2 ** (lvl - 1)
            lo = SUBLANES * lvl
            dst = lvl_sc.at[lvl % 2]
            dst[lo:halo + tile, :] = (src[lo:halo + tile, src_cols]
                                      + src[lo - sh:halo + tile - sh, src_cols])
            src, src_cols = dst, slice(None)
        sh = 2 ** (n_lvl - 1)
        wsum = src[halo:halo + tile, src_cols] + src[halo - sh:halo + tile - sh, src_cols]
        cnt = jnp.minimum(pos + 1, w).astype(F32)
        p = wsum / cnt - pool_in[:, cols]
        yb = jnp.dot(p.astype(BF16), pw_ref[g], preferred_element_type=F32) * ps_ref[:, cols]
        ys.append(yb.astype(BF16))
    y = jnp.concatenate(ys, axis=-1)
    o_ref[...] = h + jnp.dot(y, wout_ref[...], preferred_element_type=F32)


def _even_layer(h, layer, e, g, w_in, conv_w, pool_w, pool_scale, w_out):
    bsz, seq, d = h.shape
    tile = min(2 * TOKEN_TILE, seq)
    assert seq % tile == 0 and tile >= POOL_HALO
    act_spec = pl.BlockSpec((None, tile, d), lambda b, s: (b, s, 0))
    return pl.pallas_call(
        functools.partial(_even_kernel, tile=tile),
        grid=(bsz, seq // tile),
        in_specs=[act_spec, _layer_spec(g.shape, layer), _layer_spec(w_in.shape, e),
                  _layer_spec(conv_w.shape, e), _layer_spec(pool_w.shape, e),
                  _layer_spec(pool_scale.shape, e), _layer_spec(w_out.shape, e)],
        out_specs=act_spec,
        out_shape=jax.ShapeDtypeStruct(h.shape, F32),
        scratch_shapes=[pltpu.VMEM((SUBLANES + tile, A_W), F32),
                        pltpu.VMEM((POOL_HALO + tile, B_W), F32),
                        pltpu.VMEM((2, POOL_HALO + tile, POOL_G), F32),
                        pltpu.VMEM((SUBLANES, A_W), F32),
                        pltpu.VMEM((POOL_HALO, B_W), F32)],
        compiler_params=_params(("arbitrary", "arbitrary")),
        name="even_mixer",
    )(h, g, w_in, conv_w, pool_w, pool_scale, w_out)


def _ffn_kernel(h_ref, g_ref, win_ref, cw_ref, cb_ref, wout_ref, fg_ref, *rest, tile, final_norm):
    if len(rest) == 3:
        o_ref, ext_sc, tail_sc = rest
    else:
        next_win_ref, next_wout_ref, o_ref, next_win_o, next_wout_o, ext_sc, tail_sc = rest
        next_win_o[...] = next_win_ref[...].astype(BF16)
        next_wout_o[...] = next_wout_ref[...].astype(BF16)
    s = pl.program_id(1)

    @pl.when(s == 0)
    def _():
        tail_sc[...] = jnp.zeros_like(tail_sc)

    h = h_ref[...]
    hn = _rmsnorm(h, g_ref[...]).astype(BF16)
    acc = h
    for c in range(D_FF // FFN_CHUNK):
        cols = slice(c * FFN_CHUNK, (c + 1) * FFN_CHUNK)
        gcols = slice(D_FF + c * FFN_CHUNK, D_FF + (c + 1) * FFN_CHUNK)
        u = jnp.dot(hn, win_ref[:, cols], preferred_element_type=F32)
        gt = jnp.dot(hn, win_ref[:, gcols], preferred_element_type=F32)
        ext_sc[0:SUBLANES, :] = tail_sc[:, cols]
        ext_sc[SUBLANES:SUBLANES + tile, :] = u
        tail_sc[:, cols] = u[tile - SUBLANES:, :]
        a = (cw_ref[0:1, cols] * ext_sc[SUBLANES - 2:SUBLANES - 2 + tile, :]
             + cw_ref[1:2, cols] * ext_sc[SUBLANES - 1:SUBLANES - 1 + tile, :]
             + cw_ref[2:3, cols] * u
             + cb_ref[:, cols])
        act = a * (1.0 / (1.0 + jnp.exp(-a))) * gt
        acc = acc + jnp.dot(act.astype(BF16), wout_ref[cols, :], preferred_element_type=F32)
    if final_norm:
        acc = _rmsnorm(acc, fg_ref[...])
    o_ref[...] = acc


def _ffn_layer(h, layer, g, w_in, w_out, w_index, conv_w, conv_b, final_g, final_norm,
               next_w_in=None, next_w_out=None):
    bsz, seq, d = h.shape
    tile = min(TOKEN_TILE, seq)
    assert seq % tile == 0
    n_seq = seq // tile
    act_spec = pl.BlockSpec((None, tile, d), lambda b, s: (b, s, 0))
    in_specs = [act_spec, _layer_spec(g.shape, layer), _layer_spec(w_in.shape, w_index),
                _layer_spec(conv_w.shape, layer), _layer_spec(conv_b.shape, layer),
                _layer_spec(w_out.shape, w_index), _const_spec((1, d))]
    operands = [h, g, w_in, conv_w, conv_b, w_out, final_g.reshape(1, d)]
    out_specs, out_shape = [act_spec], [jax.ShapeDtypeStruct(h.shape, F32)]
    if next_w_in is not None:
        steps = bsz * n_seq
        out_chunks = math.gcd(steps, D_FF // BF16_ROWS)
        rows_in, rows_out = d // steps, D_FF // out_chunks
        assert d % steps == 0 and rows_in % BF16_ROWS == 0 and rows_out % BF16_ROWS == 0
        rep = steps // out_chunks
        in_specs += [pl.BlockSpec((None, rows_in, 2 * D_FF), lambda b, s: (layer + 1, b * n_seq + s, 0)),
                     pl.BlockSpec((None, rows_out, d),
                                  lambda b, s: (layer + 1, (b * n_seq + s) // rep, 0))]
        operands += [next_w_in, next_w_out]
        out_specs += [pl.BlockSpec((None, rows_in, 2 * D_FF), lambda b, s: (0, b * n_seq + s, 0)),
                      pl.BlockSpec((None, rows_out, d), lambda b, s: (0, (b * n_seq + s) // rep, 0))]
        out_shape += [jax.ShapeDtypeStruct((1, d, 2 * D_FF), BF16),
                      jax.ShapeDtypeStruct((1, D_FF, d), BF16)]
    return pl.pallas_call(
        functools.partial(_ffn_kernel, tile=tile, final_norm=final_norm),
        grid=(bsz, n_seq),
        in_specs=in_specs,
        out_specs=out_specs,
        out_shape=out_shape,
        scratch_shapes=[pltpu.VMEM((SUBLANES + tile, FFN_CHUNK), F32),
                        pltpu.VMEM((SUBLANES, D_FF), F32)],
        compiler_params=_params(("arbitrary", "arbitrary")),
        name="conv_ffn",
    )(*operands)


_NT_DIMS = (((1,), (1,)), ((), ()))


def _qkv_kernel(h_ref, g_ref, wqt_ref, wk_ref, wvt_ref, wf_ref, qt_ref, k_ref, vt_ref, zf_ref):
    hn = _rmsnorm(h_ref[...], g_ref[...]).astype(BF16)
    qt = lax.dot_general(wqt_ref[...], hn, _NT_DIMS, preferred_element_type=F32)
    qt_ref[...] = (qt * (ATTN_SCALE * LOG2E)).astype(BF16)
    k_ref[...] = jnp.dot(hn, wk_ref[...], preferred_element_type=F32).astype(BF16)
    vt_ref[...] = lax.dot_general(wvt_ref[...], hn, _NT_DIMS, preferred_element_type=F32).astype(BF16)
    zf_ref[...] = jnp.dot(hn, wf_ref[...], preferred_element_type=F32)


def _qkv_proj(h, g, w_in):
    bsz, seq, d = h.shape
    tile = min(2 * TOKEN_TILE, seq)
    assert seq % tile == 0
    w_qt = w_in[:, :ATT_W].T.astype(BF16)
    w_k = w_in[:, ATT_W:2 * ATT_W].astype(BF16)
    w_vt = w_in[:, 2 * ATT_W:3 * ATT_W].T.astype(BF16)
    w_f = jnp.pad(w_in[:, 3 * ATT_W:], ((0, 0), (0, LANES - H_FOX))).astype(BF16)
    act_spec = pl.BlockSpec((None, tile, d), lambda b, s: (b, s, 0))
    tok_major = pl.BlockSpec((None, tile, ATT_W), lambda b, s: (b, s, 0))
    feat_major = pl.BlockSpec((None, ATT_W, tile), lambda b, s: (b, 0, s))
    zf_spec = pl.BlockSpec((None, tile, LANES), lambda b, s: (b, s, 0))
    return pl.pallas_call(
        _qkv_kernel,
        grid=(bsz, seq // tile),
        in_specs=[act_spec, _const_spec((1, d)), _const_spec(w_qt.shape), _const_spec(w_k.shape),
                  _const_spec(w_vt.shape), _const_spec(w_f.shape)],
        out_specs=[feat_major, tok_major, feat_major, zf_spec],
        out_shape=[jax.ShapeDtypeStruct((bsz, ATT_W, seq), BF16),
                   jax.ShapeDtypeStruct((bsz, seq, ATT_W), BF16),
                   jax.ShapeDtypeStruct((bsz, ATT_W, seq), BF16),
                   jax.ShapeDtypeStruct((bsz, seq, LANES), F32)],
        compiler_params=_params(("arbitrary", "arbitrary")),
        name="qkv_proj",
    )(h, g.reshape(1, d), w_qt, w_k, w_vt, w_f)


def _fgate_kernel(zf_ref, bf_ref, o_ref, *, seq):
    z = zf_ref[...] + bf_ref[...]
    x = -(jnp.maximum(-z, 0.0) + jnp.log1p(jnp.exp(-jnp.abs(z))))
    row = lax.broadcasted_iota(jnp.int32, (seq, LANES), 0)
    sh = 1
    while sh < seq:
        x = x + jnp.where(row >= sh, pltpu.roll(x, sh, 0), 0.0)
        sh *= 2
    x = x * LOG2E
    pieces = []
    for _ in range(F_PIECES):
        piece = x.astype(BF16)
        pieces.append(piece)
        x = x - piece.astype(F32)
    src = lax.broadcasted_iota(jnp.int32, (F_PIECES * LANES, N_PAIR * LANES), 0)
    dst = lax.broadcasted_iota(jnp.int32, (F_PIECES * LANES, N_PAIR * LANES), 1)
    piece_id, head = src // LANES, src % LANES
    pair, slot = dst // LANES, dst % LANES
    hit = (head // 2 == pair) & (slot == F_PIECES * (head % 2) + piece_id) & (head < H_FOX)
    place = jnp.where(hit, 1.0, 0.0).astype(BF16)
    o_ref[...] = jnp.dot(jnp.concatenate(pieces, axis=-1), place,
                         preferred_element_type=F32).astype(BF16)


def _fgate_cumsum(zf, b_f):
    bsz, seq, _ = zf.shape
    return pl.pallas_call(
        functools.partial(_fgate_kernel, seq=seq),
        grid=(bsz,),
        in_specs=[pl.BlockSpec((None, seq, LANES), lambda b: (b, 0, 0)), _const_spec((1, LANES))],
        out_specs=pl.BlockSpec((None, seq, N_PAIR * LANES), lambda b: (b, 0, 0)),
        out_shape=jax.ShapeDtypeStruct((bsz, seq, N_PAIR * LANES), BF16),
        compiler_params=_params(("arbitrary",)),
        name="fgate_cumsum",
    )(zf, jnp.pad(b_f, (0, LANES - H_FOX)).reshape(1, LANES))


ACC_ROWS = HEAD_DIM + BF16_ROWS


def _fill_value_rows(vaug_sc, vt_ref, n_heads):
    ones = jnp.ones((BF16_ROWS, vt_ref.shape[1]), BF16)
    for h in range(n_heads):
        vaug_sc[h, 0:HEAD_DIM, :] = vt_ref[h * HEAD_DIM:(h + 1) * HEAD_DIM, :]
        vaug_sc[h, HEAD_DIM:, :] = ones


def _dual_pipeline(prefix, loop_tile, n_loop, extra, tk, n_heads, vaug_sc, slots, m_sc, acc_sc):
    def stage_q(tile, slot):
        s_sc, t_sc, _, _ = slots[slot]
        for h, s_t in enumerate(tile[0]()):
            s_sc[h] = s_t
            t_sc[h] = jnp.max(s_t, axis=0, keepdims=True)

    def stage_x(slot, state):
        s_sc, t_sc, p_sc, a_sc = slots[slot]
        for h in range(n_heads):
            m_prev = m_sc[state, h]
            m_new = jnp.maximum(m_prev, t_sc[h])
            a_sc[h] = jnp.exp2(m_prev - m_new)
            p_sc[h] = jnp.exp2(s_sc[h] - m_new).astype(BF16)
            m_sc[state, h] = m_new

    def stage_v(k0, slot, state):
        _, _, p_sc, a_sc = slots[slot]
        for h in range(n_heads):
            pv = jnp.dot(vaug_sc[h, :, pl.ds(k0, tk)], p_sc[h], preferred_element_type=F32)
            acc_sc[state, h] = a_sc[h] * acc_sc[state, h] + pv

    assert len(prefix) % 2 == 0
    m_sc[...] = jnp.full_like(m_sc, MASK_VALUE)
    acc_sc[...] = jnp.zeros_like(acc_sc)
    stage_q(prefix[0], 0)
    stage_q(prefix[1], 1)
    stage_x(0, 0)
    for u in range(1, len(prefix) - 1):
        stage_q(prefix[u + 1], (u + 1) % 2)
        stage_v(prefix[u - 1][1], (u - 1) % 2, (u - 1) % 2)
        stage_x(u % 2, u % 2)

    def pending_k0(state, k):
        return jnp.where(k == 0, prefix[len(prefix) - 2 + state][1],
                         loop_tile(state, jnp.maximum(k - 1, 0))[1])

    def pair_body(k, carry):
        stage_q(loop_tile(0, k), 0)
        stage_v(pending_k0(0, k), 0, 0)
        stage_x(1, 1)
        stage_q(loop_tile(1, k), 1)
        stage_v(pending_k0(1, k), 1, 1)
        stage_x(0, 0)
        return carry

    lax.fori_loop(0, n_loop, pair_body, 0)
    stage_q(extra, 0)
    stage_v(pending_k0(0, n_loop), 0, 0)
    stage_x(1, 1)
    stage_v(pending_k0(1, n_loop), 1, 1)
    stage_x(0, 1)
    stage_v(extra[1], 0, 1)


def _pipeline_scratch(n_heads, tile):
    one = [pltpu.VMEM((n_heads, tile, tile), F32), pltpu.VMEM((n_heads, 1, tile), F32),
           pltpu.VMEM((n_heads, tile, tile), BF16), pltpu.VMEM((n_heads, 1, tile), F32)]
    return one + one


def _softmax_finish(acc_sc, n_heads):
    blocks = []
    for pair in range(n_heads // 2):
        halves = []
        for h in (2 * pair, 2 * pair + 1):
            acc = acc_sc[h]
            halves.append(acc[0:HEAD_DIM, :] / acc[HEAD_DIM:HEAD_DIM + 1, :])
        out_t = jnp.concatenate(halves, axis=0)
        blocks.append(jnp.transpose(out_t).astype(BF16))
    return blocks


def _head_rows(qt_ref, h, q0, tq):
    pair, e = divmod(h, 2)
    q = qt_ref[pair * HEAD_PAIR_W:(pair + 1) * HEAD_PAIR_W, pl.ds(q0, tq)]
    row = lax.broadcasted_iota(jnp.int32, q.shape, 0)
    mine = (row >= e * HEAD_DIM) & (row < (e + 1) * HEAD_DIM)
    return jnp.where(mine, q, jnp.zeros_like(q))


def _key_le_query(tile):
    key = lax.broadcasted_iota(jnp.int32, (tile, tile), 0)
    query = lax.broadcasted_iota(jnp.int32, (tile, tile), 1)
    return key <= query


def _fox_kernel(qt_ref, k_ref, f_ref, vt_ref, o_ref,
                kaug_sc, vaug_sc, m_sc, acc_sc, *slot_refs, tile):
    g = pl.program_id(1)

    @pl.when(g == 0)
    def _():
        for pair in range(N_PAIR):
            kaug_sc[pair, :, 0:HEAD_PAIR_W] = k_ref[:, pair * HEAD_PAIR_W:(pair + 1) * HEAD_PAIR_W]
            kaug_sc[pair, :, HEAD_PAIR_W:] = f_ref[:, pair * LANES:(pair + 1) * LANES]
        _fill_value_rows(vaug_sc, vt_ref, H_FOX)

    row = lax.broadcasted_iota(jnp.int32, (LANES, tile), 0)
    q_aug = []
    for state in range(2):
        q_aug.append([])
        for h in range(H_FOX):
            e = h % 2
            minus_one = (row >= F_PIECES * e) & (row < F_PIECES * (e + 1))
            sel = jnp.where(minus_one, -1.0, 0.0).astype(BF16)
            q_aug[state].append(jnp.concatenate([_head_rows(qt_ref, h, state * tile, tile), sel],
                                                axis=0))

    def key_tile(state, j, masked):
        k0 = pl.multiple_of(j * tile, tile)

        def scores():
            tiles = [jnp.dot(kaug_sc[h // 2, pl.ds(k0, tile), :], q_aug[state][h],
                             preferred_element_type=F32) for h in range(H_FOX)]
            if masked:
                allowed = _key_le_query(tile)
                tiles = [jnp.where(allowed, s_t, MASK_VALUE) for s_t in tiles]
            return tiles
        return scores, k0

    _dual_pipeline(prefix=[key_tile(0, 2 * g, True), key_tile(1, 2 * g + 1, True)],
                   loop_tile=lambda state, k: key_tile(state, k, False), n_loop=2 * g,
                   extra=key_tile(1, 2 * g, False), tk=tile, n_heads=H_FOX, vaug_sc=vaug_sc,
                   slots=(slot_refs[:4], slot_refs[4:]), m_sc=m_sc, acc_sc=acc_sc)
    for state in range(2):
        for pair, block in enumerate(_softmax_finish(acc_sc.at[state], H_FOX)):
            o_ref[state * tile:(state + 1) * tile, pair * HEAD_PAIR_W:(pair + 1) * HEAD_PAIR_W] = block


def _fox_attention(qt, k, vt, faug):
    bsz, seq, _ = k.shape
    tile = min(ATT_TILE, seq)
    step = 2 * tile
    assert seq % step == 0 and tile % LANES == 0
    width = H_FOX * HEAD_DIM
    return pl.pallas_call(
        functools.partial(_fox_kernel, tile=tile),
        grid=(bsz, seq // step),
        in_specs=[pl.BlockSpec((None, width, step), lambda b, g: (b, 0, g)),
                  pl.BlockSpec((None, seq, width), lambda b, g: (b, 0, 0)),
                  pl.BlockSpec((None, seq, N_PAIR * LANES), lambda b, g: (b, 0, 0)),
                  pl.BlockSpec((None, width, seq), lambda b, g: (b, 0, 0))],
        out_specs=pl.BlockSpec((None, step, width), lambda b, g: (b, g, 0)),
        out_shape=jax.ShapeDtypeStruct((bsz, seq, width), BF16),
        scratch_shapes=[pltpu.VMEM((N_PAIR, seq, 2 * HEAD_PAIR_W), BF16),
                        pltpu.VMEM((H_FOX, ACC_ROWS, seq), BF16),
                        pltpu.VMEM((2, H_FOX, 1, tile), F32),
                        pltpu.VMEM((2, H_FOX, ACC_ROWS, tile), F32)]
        + _pipeline_scratch(H_FOX, tile),
        compiler_params=_params(("arbitrary", "arbitrary")),
        name="fox_attention",
    )(qt, k, faug, vt)


def _t5_bias_tile(dist, rel_ref, head):
    exact = T5_BUCKETS // 2
    d_f = jnp.maximum(dist, 1).astype(F32)
    log_b = exact + (jnp.log(d_f / exact) / math.log(T5_MAX_DIST / exact)
                     * (T5_BUCKETS - exact)).astype(jnp.int32)
    log_b = jnp.minimum(log_b, T5_BUCKETS - 1)
    bucket = jnp.where(dist < exact, dist, log_b)
    out = jnp.zeros(dist.shape, F32)
    for b in range(T5_BUCKETS):
        out = jnp.where(bucket == b, rel_ref[b, head], out)
    return out


def _moba_kernel(rel_ref, qt_ref, k_ref, vt_ref, h_ref, yc_ref, wo_ref, o_ref,
                 kmean_sc, vaug_sc, bias_sc, pen_sc, m_sc, acc_sc, *slot_refs, seq, blk_rows):
    blk = MOBA_BLOCK
    b = pl.program_id(0)
    g = pl.program_id(1)

    @pl.when((b == 0) & (g == 0))
    def _():
        key = lax.broadcasted_iota(jnp.int32, (blk, blk), 0)
        query = lax.broadcasted_iota(jnp.int32, (blk, blk), 1)
        for h in range(H_MOBA):
            for delta in range(2):
                dist = jnp.maximum(delta * blk + query - key, 0)
                bias_sc[h, delta] = _t5_bias_tile(dist, rel_ref, h) * LOG2E

    @pl.when(g == 0)
    def _():
        mrow = lax.broadcasted_iota(jnp.int32, (blk_rows, seq), 0)
        mcol = lax.broadcasted_iota(jnp.int32, (blk_rows, seq), 1) // blk
        avg = jnp.where(mrow == mcol, 1.0 / blk, 0.0).astype(BF16)
        kmean_sc[...] = jnp.dot(avg, k_ref[...], preferred_element_type=F32)
        _fill_value_rows(vaug_sc, vt_ref, H_MOBA)

    own = (2 * g, 2 * g + 1)
    qh = [[_head_rows(qt_ref, h, state * blk, blk) for h in range(H_MOBA)] for state in range(2)]

    blk_id = lax.broadcasted_iota(jnp.int32, (blk_rows, blk), 0)
    q_far = [[], []]
    for state in range(2):
        for h in range(H_MOBA):
            pair = h // 2
            kmean = kmean_sc[:, pair * HEAD_PAIR_W:(pair + 1) * HEAD_PAIR_W]
            km_hi = kmean.astype(BF16)
            km_lo = (kmean - km_hi.astype(F32)).astype(BF16)
            gate = (jnp.dot(km_hi, qh[state][h], preferred_element_type=F32)
                    + jnp.dot(km_lo, qh[state][h], preferred_element_type=F32))
            gate = jnp.where(blk_id < own[state], gate, -jnp.inf)
            sel = jnp.zeros(gate.shape, jnp.bool_)
            for _ in range(MOBA_TOPK):
                best = jnp.max(gate, axis=0, keepdims=True)
                first_id = jnp.min(jnp.where(gate == best, blk_id, blk_rows), axis=0, keepdims=True)
                pick = (blk_id == first_id) & (best > -jnp.inf)
                sel = sel | pick
                gate = jnp.where(pick, -jnp.inf, gate)
            pen = jnp.where(sel, 0.0, MASK_VALUE)
            pen_sc[state, h] = pen
            far = pen + rel_ref[T5_BUCKETS - 1, h] * LOG2E
            far_hi = far.astype(BF16)
            far_lo = (far - far_hi.astype(F32)).astype(BF16)
            pad = jnp.zeros((HEAD_PAIR_W - 2 * blk_rows, blk), BF16)
            q_far[state].append(jnp.concatenate([qh[state][h], far_hi, far_lo, pad], axis=0))

    def key_rows(j, h):
        pair = h // 2
        return k_ref[pl.ds(pl.multiple_of(j * blk, blk), blk),
                     pair * HEAD_PAIR_W:(pair + 1) * HEAD_PAIR_W]

    def own_block(state):
        j = own[state]

        def scores():
            allowed = _key_le_query(blk)
            return [jnp.where(allowed, jnp.dot(key_rows(j, h), qh[state][h],
                                               preferred_element_type=F32) + bias_sc[h, 0],
                              MASK_VALUE) for h in range(H_MOBA)]
        return scores, pl.multiple_of(j * blk, blk)

    def previous_block(state):
        j = jnp.maximum(own[state] - 1, 0)

        def scores():
            return [jnp.dot(key_rows(j, h), qh[state][h], preferred_element_type=F32)
                    + bias_sc[h, 1] + pen_sc[state, h, pl.ds(j, 1), :] for h in range(H_MOBA)]
        return scores, pl.multiple_of(j * blk, blk)

    def older_block(state, j, present=None):
        def scores():
            lane = lax.broadcasted_iota(jnp.int32, (blk, HEAD_PAIR_W), 1)
            one_hot = jnp.where((lane == j) | (lane == blk_rows + j), 1.0, 0.0).astype(BF16)
            tiles = [jnp.dot(jnp.concatenate([key_rows(j, h), one_hot], axis=1), q_far[state][h],
                             preferred_element_type=F32) for h in range(H_MOBA)]
            if present is not None:
                absent = jnp.where(present, 0.0, MASK_VALUE)
                tiles = [s_t + absent for s_t in tiles]
            return tiles
        return scores, pl.multiple_of(j * blk, blk)

    _dual_pipeline(prefix=[own_block(0), own_block(1), previous_block(0), previous_block(1)],
                   loop_tile=lambda state, k: older_block(state, k),
                   n_loop=jnp.maximum(2 * g - 1, 0),
                   extra=older_block(1, jnp.maximum(2 * g - 1, 0), present=g >= 1),
                   tk=blk, n_heads=H_MOBA, vaug_sc=vaug_sc,
                   slots=(slot_refs[:4], slot_refs[4:]), m_sc=m_sc, acc_sc=acc_sc)

    wc = yc_ref.shape[-1]
    for state in range(2):
        y_d = jnp.concatenate(_softmax_finish(acc_sc.at[state], H_MOBA), axis=-1)
        rows = slice(state * blk, (state + 1) * blk)
        o_ref[rows, :] = (h_ref[rows, :]
                          + jnp.dot(yc_ref[rows, :], wo_ref[0:wc, :], preferred_element_type=F32)
                          + jnp.dot(y_d, wo_ref[wc:, :], preferred_element_type=F32))


def _moba_attention_out(qt, k, vt, rel_bias, h, y_c, w_out, o):
    bsz, seq, d = h.shape
    blk = MOBA_BLOCK
    step = 2 * blk
    assert seq % step == 0
    assert blk + 1 >= T5_MAX_DIST
    blk_rows = -(-(seq // blk) // BF16_ROWS) * BF16_ROWS
    width = H_MOBA * HEAD_DIM
    off = (H_FOX * HEAD_DIM) // width
    return pl.pallas_call(
        functools.partial(_moba_kernel, seq=seq, blk_rows=blk_rows),
        grid_spec=pltpu.PrefetchScalarGridSpec(
            num_scalar_prefetch=1,
            grid=(bsz, seq // step),
            in_specs=[pl.BlockSpec((None, width, step), lambda b, g, rel: (b, off, g)),
                      pl.BlockSpec((None, seq, width), lambda b, g, rel: (b, 0, off)),
                      pl.BlockSpec((None, width, seq), lambda b, g, rel: (b, off, 0)),
                      pl.BlockSpec((None, step, d), lambda b, g, rel: (b, g, 0)),
                      pl.BlockSpec((None, step, y_c.shape[-1]), lambda b, g, rel: (b, g, 0)),
                      pl.BlockSpec((None,) + w_out.shape[1:], lambda b, g, rel: (o, 0, 0),
                                   pipeline_mode=pl.Buffered(1))],
            out_specs=pl.BlockSpec((None, step, d), lambda b, g, rel: (b, g, 0)),
            scratch_shapes=[pltpu.VMEM((blk_rows, width), F32),
                            pltpu.VMEM((H_MOBA, ACC_ROWS, seq), BF16),
                            pltpu.VMEM((H_MOBA, 2, blk, blk), F32),
                            pltpu.VMEM((2, H_MOBA, blk_rows, blk), F32),
                            pltpu.VMEM((2, H_MOBA, 1, blk), F32),
                            pltpu.VMEM((2, H_MOBA, ACC_ROWS, blk), F32)]
            + _pipeline_scratch(H_MOBA, blk)),
        out_shape=jax.ShapeDtypeStruct(h.shape, F32),
        compiler_params=_params(("arbitrary", "arbitrary")),
        name="moba_attention_out",
    )(rel_bias, qt, k, vt, h, y_c, w_out)


def kernel(x, mix_norm_g, ffn_norm_g, final_norm_g, ev_w_in, ev_conv_w, ev_pool_w, ev_pool_scale,
           ev_w_out, od_w_in, od_b_f, od_w_out, rel_bias, ffn_w_in, ffn_conv_w, ffn_conv_b, ffn_w_out):
    depth, d = mix_norm_g.shape
    mix_g = mix_norm_g.reshape(depth, 1, d)
    ffn_g = ffn_norm_g.reshape(depth, 1, d)
    ev_w_in_b, ev_pool_w_b, ev_w_out_b = (w.astype(BF16) for w in (ev_w_in, ev_pool_w, ev_w_out))
    ev_scale = ev_pool_scale.reshape(ev_pool_scale.shape[0], 1, B_W)
    od_w_out_b = od_w_out.astype(BF16)
    ffn_w_in_b, ffn_w_out_b = ffn_w_in[0:1].astype(BF16), ffn_w_out[0:1].astype(BF16)
    ffn_b = ffn_conv_b.reshape(depth, 1, D_FF)
    h = x
    for layer in range(depth):
        if layer % 2 == 0:
            h = _even_layer(h, layer, layer // 2, mix_g, ev_w_in_b, ev_conv_w, ev_pool_w_b, ev_scale,
                            ev_w_out_b)
        else:
            o = layer // 2
            qt, k, vt, zf = _qkv_proj(h, mix_norm_g[layer], od_w_in[o])
            faug = _fgate_cumsum(zf, od_b_f[o])
            y_c = _fox_attention(qt, k, vt, faug)
            h = _moba_attention_out(qt, k, vt, rel_bias, h, y_c, od_w_out_b, o)
        last = layer == depth - 1
        outs = _ffn_layer(h, layer, ffn_g, ffn_w_in_b, ffn_w_out_b, 0, ffn_conv_w, ffn_b, final_norm_g,
                          final_norm=last, next_w_in=None if last else ffn_w_in,
                          next_w_out=None if last else ffn_w_out)
        h = outs[0]
        if not last:
            ffn_w_in_b, ffn_w_out_b = outs[1], outs[2]
    return h
```

```python
import functools
import math

import jax
import jax.numpy as jnp
from jax import lax
from jax.experimental import pallas as pl
from jax.experimental.pallas import tpu as pltpu

F32 = jnp.float32
BF16 = jnp.bfloat16

D_MODEL = 1024
HEAD_DIM = 64
CONV_W = 3
A_W = D_MODEL // 2
B_W = D_MODEL - A_W
POOL_WINDOWS = (2, 4, 8, 16)
POOL_G = B_W // len(POOL_WINDOWS)
H_FOX = D_MODEL // (2 * HEAD_DIM)
H_MOBA = D_MODEL // (2 * HEAD_DIM)
ATT_W = (H_FOX + H_MOBA) * HEAD_DIM
ATTN_SCALE = HEAD_DIM ** -0.5
MOBA_BLOCK = 256
MOBA_TOPK = 3
T5_BUCKETS = 32
T5_MAX_DIST = 128
D_FF = 2816
RMS_EPS = 1e-6

LANES = 128
SUBLANES = 8
BF16_ROWS = 16
VMEM_LIMIT_BYTES = 56 * 1024 * 1024

HEAD_PAIR_W = 2 * HEAD_DIM
N_PAIR = H_FOX // 2
MASK_VALUE = -1e30
LOG2E = math.log2(math.e)
F_PIECES = 3

TOKEN_TILE = 512
FFN_CHUNK = D_FF
ATT_TILE = MOBA_BLOCK
POOL_HALO = 32


def _rmsnorm(x, g):
    ms = jnp.mean(x * x, axis=-1, keepdims=True)
    return x * lax.rsqrt(ms + RMS_EPS) * g


def _const_spec(shape):
    nd = len(shape)
    return pl.BlockSpec(shape, lambda *_: (0,) * nd, pipeline_mode=pl.Buffered(1))


def _layer_spec(stacked_shape, index):
    nd = len(stacked_shape)
    return pl.BlockSpec((None,) + tuple(stacked_shape[1:]), lambda *_: (index,) + (0,) * (nd - 1),
                        pipeline_mode=pl.Buffered(1))


def _side_cast(ffn_w_in, ffn_w_out, src_layer, n_outer, n_inner):
    d, steps = ffn_w_in.shape[1], n_outer * n_inner
    out_chunks = math.gcd(steps, D_FF // BF16_ROWS)
    rows_in, rows_out = d // steps, D_FF // out_chunks
    assert d % steps == 0 and rows_in % BF16_ROWS == 0 and rows_out % BF16_ROWS == 0
    rep = steps // out_chunks
    in_specs = [pl.BlockSpec((None, rows_in, 2 * D_FF), lambda b, s: (src_layer, b * n_inner + s, 0)),
                pl.BlockSpec((None, rows_out, d), lambda b, s: (src_layer, (b * n_inner + s) // rep, 0))]
    out_specs = [pl.BlockSpec((None, rows_in, 2 * D_FF), lambda b, s: (0, b * n_inner + s, 0)),
                 pl.BlockSpec((None, rows_out, d), lambda b, s: (0, (b * n_inner + s) // rep, 0))]
    out_shapes = [jax.ShapeDtypeStruct((1, d, 2 * D_FF), BF16), jax.ShapeDtypeStruct((1, D_FF, d), BF16)]
    return in_specs, out_specs, out_shapes


def _params(semantics):
    return pltpu.CompilerParams(dimension_semantics=semantics, vmem_limit_bytes=VMEM_LIMIT_BYTES)


def _even_kernel(h_ref, g_ref, win_ref, cw_ref, pw_ref, ps_ref, wout_ref, *rest, tile):
    if len(rest) == 6:
        o_ref, cv_sc, u_sc, lvl_sc, tailcv_sc, tailu_sc = rest
    else:
        (ffn_win_ref, ffn_wout_ref, o_ref, ffn_win_o, ffn_wout_o,
         cv_sc, u_sc, lvl_sc, tailcv_sc, tailu_sc) = rest
        ffn_win_o[...] = ffn_win_ref[...].astype(BF16)
        ffn_wout_o[...] = ffn_wout_ref[...].astype(BF16)
    s = pl.program_id(1)
    halo = POOL_HALO

    @pl.when(s == 0)
    def _():
        tailcv_sc[...] = jnp.zeros_like(tailcv_sc)
        tailu_sc[...] = jnp.zeros_like(tailu_sc)

    h = h_ref[...]
    hn = _rmsnorm(h, g_ref[...]).astype(BF16)
    z = jnp.dot(hn, win_ref[...], preferred_element_type=F32)
    gate_b = z[:, :A_W]
    cv = z[:, A_W:2 * A_W] * z[:, 2 * A_W:3 * A_W]
    pool_in = z[:, 3 * A_W:]

    cv_sc[0:SUBLANES, :] = tailcv_sc[...]
    cv_sc[SUBLANES:SUBLANES + tile, :] = cv
    tailcv_sc[...] = cv[tile - SUBLANES:, :]
    conv = (cw_ref[0:1, :] * cv_sc[SUBLANES - 2:SUBLANES - 2 + tile, :]
            + cw_ref[1:2, :] * cv_sc[SUBLANES - 1:SUBLANES - 1 + tile, :]
            + cw_ref[2:3, :] * cv)
    y_a = gate_b * conv

    u_sc[0:halo, :] = tailu_sc[...]
    u_sc[halo:halo + tile, :] = pool_in
    tailu_sc[...] = pool_in[tile - halo:, :]
    pos = s * tile + lax.broadcasted_iota(jnp.int32, (tile, 1), 0)
    ys = [y_a.astype(BF16)]
    for g, w in enumerate(POOL_WINDOWS):
        cols = slice(g * POOL_G, (g + 1) * POOL_G)
        n_lvl = g + 1
        src, src_cols = u_sc, cols
        for lvl in range(1, n_lvl):
            sh = 2 ** (lvl - 1)
            lo = SUBLANES * lvl
            dst = lvl_sc.at[lvl % 2]
            dst[lo:halo + tile, :] = (src[lo:halo + tile, src_cols]
                                      + src[lo - sh:halo + tile - sh, src_cols])
            src, src_cols = dst, slice(None)
        sh = 2 ** (n_lvl - 1)
        wsum = src[halo:halo + tile, src_cols] + src[halo - sh:halo + tile - sh, src_cols]
        cnt = jnp.minimum(pos + 1, w).astype(F32)
        p = wsum / cnt - pool_in[:, cols]
        yb = jnp.dot(p.astype(BF16), pw_ref[g], preferred_element_type=F32) * ps_ref[:, cols]
        ys.append(yb.astype(BF16))
    y = jnp.concatenate(ys, axis=-1)
    o_ref[...] = h + jnp.dot(y, wout_ref[...], preferred_element_type=F32)


def _even_layer(h, layer, e, g, w_in, conv_w, pool_w, pool_scale, w_out, ffn_w=None):
    bsz, seq, d = h.shape
    tile = min(2 * TOKEN_TILE, seq)
    assert seq % tile == 0 and tile >= POOL_HALO
    act_spec = pl.BlockSpec((None, tile, d), lambda b, s: (b, s, 0))
    in_specs = [act_spec, _layer_spec(g.shape, layer), _layer_spec(w_in.shape, e),
                _layer_spec(conv_w.shape, e), _layer_spec(pool_w.shape, e),
                _layer_spec(pool_scale.shape, e), _layer_spec(w_out.shape, e)]
    operands = [h, g, w_in, conv_w, pool_w, pool_scale, w_out]
    out_specs, out_shape = [act_spec], [jax.ShapeDtypeStruct(h.shape, F32)]
    if ffn_w is not None:
        extra_in, extra_out, extra_shapes = _side_cast(*ffn_w, layer, bsz, seq // tile)
        in_specs, out_specs, out_shape = in_specs + extra_in, out_specs + extra_out, out_shape + extra_shapes
        operands += list(ffn_w)
    return pl.pallas_call(
        functools.partial(_even_kernel, tile=tile),
        grid=(bsz, seq // tile),
        in_specs=in_specs,
        out_specs=out_specs,
        out_shape=out_shape,
        scratch_shapes=[pltpu.VMEM((SUBLANES + tile, A_W), F32),
                        pltpu.VMEM((POOL_HALO + tile, B_W), F32),
                        pltpu.VMEM((2, POOL_HALO + tile, POOL_G), F32),
                        pltpu.VMEM((SUBLANES, A_W), F32),
                        pltpu.VMEM((POOL_HALO, B_W), F32)],
        compiler_params=_params(("arbitrary", "arbitrary")),
        name="even_mixer",
    )(*operands)


def _ffn_kernel(h_ref, g_ref, win_ref, cw_ref, cb_ref, wout_ref, fg_ref, *rest, tile, final_norm):
    if len(rest) == 3:
        o_ref, ext_sc, tail_sc = rest
    else:
        next_win_ref, next_wout_ref, o_ref, next_win_o, next_wout_o, ext_sc, tail_sc = rest
        next_win_o[...] = next_win_ref[...].astype(BF16)
        next_wout_o[...] = next_wout_ref[...].astype(BF16)
    s = pl.program_id(1)

    @pl.when(s == 0)
    def _():
        tail_sc[...] = jnp.zeros_like(tail_sc)

    h = h_ref[...]
    hn = _rmsnorm(h, g_ref[...]).astype(BF16)
    acc = h
    for c in range(D_FF // FFN_CHUNK):
        cols = slice(c * FFN_CHUNK, (c + 1) * FFN_CHUNK)
        gcols = slice(D_FF + c * FFN_CHUNK, D_FF + (c + 1) * FFN_CHUNK)
        u = jnp.dot(hn, win_ref[:, cols], preferred_element_type=F32)
        gt = jnp.dot(hn, win_ref[:, gcols], preferred_element_type=F32)
        ext_sc[0:SUBLANES, :] = tail_sc[:, cols]
        ext_sc[SUBLANES:SUBLANES + tile, :] = u
        tail_sc[:, cols] = u[tile - SUBLANES:, :]
        a = (cw_ref[0:1, cols] * ext_sc[SUBLANES - 2:SUBLANES - 2 + tile, :]
             + cw_ref[1:2, cols] * ext_sc[SUBLANES - 1:SUBLANES - 1 + tile, :]
             + cw_ref[2:3, cols] * u
             + cb_ref[:, cols])
        act = a * (1.0 / (1.0 + jnp.exp(-a))) * gt
        acc = acc + jnp.dot(act.astype(BF16), wout_ref[cols, :], preferred_element_type=F32)
    if final_norm:
        acc = _rmsnorm(acc, fg_ref[...])
    o_ref[...] = acc


def _ffn_layer(h, layer, g, w_in, w_out, w_index, conv_w, conv_b, final_g, final_norm,
               next_w_in=None, next_w_out=None):
    bsz, seq, d = h.shape
    tile = min(TOKEN_TILE, seq)
    assert seq % tile == 0
    n_seq = seq // tile
    act_spec = pl.BlockSpec((None, tile, d), lambda b, s: (b, s, 0))
    in_specs = [act_spec, _layer_spec(g.shape, layer), _layer_spec(w_in.shape, w_index),
                _layer_spec(conv_w.shape, layer), _layer_spec(conv_b.shape, layer),
                _layer_spec(w_out.shape, w_index), _const_spec((1, d))]
    operands = [h, g, w_in, conv_w, conv_b, w_out, final_g.reshape(1, d)]
    out_specs, out_shape = [act_spec], [jax.ShapeDtypeStruct(h.shape, F32)]
    if next_w_in is not None:
        extra_in, extra_out, extra_shapes = _side_cast(next_w_in, next_w_out, layer + 1, bsz, n_seq)
        in_specs, out_specs, out_shape = in_specs + extra_in, out_specs + extra_out, out_shape + extra_shapes
        operands += [next_w_in, next_w_out]
    return pl.pallas_call(
        functools.partial(_ffn_kernel, tile=tile, final_norm=final_norm),
        grid=(bsz, n_seq),
        in_specs=in_specs,
        out_specs=out_specs,
        out_shape=out_shape,
        scratch_shapes=[pltpu.VMEM((SUBLANES + tile, FFN_CHUNK), F32),
                        pltpu.VMEM((SUBLANES, D_FF), F32)],
        compiler_params=_params(("arbitrary", "arbitrary")),
        name="conv_ffn",
    )(*operands)


_NT_DIMS = (((1,), (1,)), ((), ()))


def _qkv_kernel(h_ref, g_ref, wqt_ref, wk_ref, wvt_ref, wf_ref, qt_ref, k_ref, vt_ref, zf_ref):
    hn = _rmsnorm(h_ref[...], g_ref[...]).astype(BF16)
    qt = lax.dot_general(wqt_ref[...], hn, _NT_DIMS, preferred_element_type=F32)
    qt_ref[...] = (qt * (ATTN_SCALE * LOG2E)).astype(BF16)
    k_ref[...] = jnp.dot(hn, wk_ref[...], preferred_element_type=F32).astype(BF16)
    vt_ref[...] = lax.dot_general(wvt_ref[...], hn, _NT_DIMS, preferred_element_type=F32).astype(BF16)
    zf_ref[...] = jnp.dot(hn, wf_ref[...], preferred_element_type=F32)


def _qkv_proj(h, g, w_in):
    bsz, seq, d = h.shape
    tile = min(2 * TOKEN_TILE, seq)
    assert seq % tile == 0
    w_qt = w_in[:, :ATT_W].T.astype(BF16)
    w_k = w_in[:, ATT_W:2 * ATT_W].astype(BF16)
    w_vt = w_in[:, 2 * ATT_W:3 * ATT_W].T.astype(BF16)
    w_f = jnp.pad(w_in[:, 3 * ATT_W:], ((0, 0), (0, LANES - H_FOX))).astype(BF16)
    act_spec = pl.BlockSpec((None, tile, d), lambda b, s: (b, s, 0))
    tok_major = pl.BlockSpec((None, tile, ATT_W), lambda b, s: (b, s, 0))
    feat_major = pl.BlockSpec((None, ATT_W, tile), lambda b, s: (b, 0, s))
    zf_spec = pl.BlockSpec((None, tile, LANES), lambda b, s: (b, s, 0))
    return pl.pallas_call(
        _qkv_kernel,
        grid=(bsz, seq // tile),
        in_specs=[act_spec, _const_spec((1, d)), _const_spec(w_qt.shape), _const_spec(w_k.shape),
                  _const_spec(w_vt.shape), _const_spec(w_f.shape)],
        out_specs=[feat_major, tok_major, feat_major, zf_spec],
        out_shape=[jax.ShapeDtypeStruct((bsz, ATT_W, seq), BF16),
                   jax.ShapeDtypeStruct((bsz, seq, ATT_W), BF16),
                   jax.ShapeDtypeStruct((bsz, ATT_W, seq), BF16),
                   jax.ShapeDtypeStruct((bsz, seq, LANES), F32)],
        compiler_params=_params(("arbitrary", "arbitrary")),
        name="qkv_proj",
    )(h, g.reshape(1, d), w_qt, w_k, w_vt, w_f)


def _fgate_kernel(zf_ref, bf_ref, o_ref, *, seq):
    z = zf_ref[...] + bf_ref[...]
    x = -(jnp.maximum(-z, 0.0) + jnp.log1p(jnp.exp(-jnp.abs(z))))
    row = lax.broadcasted_iota(jnp.int32, (seq, LANES), 0)
    sh = 1
    while sh < seq:
        x = x + jnp.where(row >= sh, pltpu.roll(x, sh, 0), 0.0)
        sh *= 2
    x = x * LOG2E
    pieces = []
    for _ in range(F_PIECES):
        piece = x.astype(BF16)
        pieces.append(piece)
        x = x - piece.astype(F32)
    src = lax.broadcasted_iota(jnp.int32, (F_PIECES * LANES, N_PAIR * LANES), 0)
    dst = lax.broadcasted_iota(jnp.int32, (F_PIECES * LANES, N_PAIR * LANES), 1)
    piece_id, head = src // LANES, src % LANES
    pair, slot = dst // LANES, dst % LANES
    hit = (head // 2 == pair) & (slot == F_PIECES * (head % 2) + piece_id) & (head < H_FOX)
    place = jnp.where(hit, 1.0, 0.0).astype(BF16)
    o_ref[...] = jnp.dot(jnp.concatenate(pieces, axis=-1), place,
                         preferred_element_type=F32).astype(BF16)


def _fgate_cumsum(zf, b_f):
    bsz, seq, _ = zf.shape
    return pl.pallas_call(
        functools.partial(_fgate_kernel, seq=seq),
        grid=(bsz,),
        in_specs=[pl.BlockSpec((None, seq, LANES), lambda b: (b, 0, 0)), _const_spec((1, LANES))],
        out_specs=pl.BlockSpec((None, seq, N_PAIR * LANES), lambda b: (b, 0, 0)),
        out_shape=jax.ShapeDtypeStruct((bsz, seq, N_PAIR * LANES), BF16),
        compiler_params=_params(("arbitrary",)),
        name="fgate_cumsum",
    )(zf, jnp.pad(b_f, (0, LANES - H_FOX)).reshape(1, LANES))


ACC_ROWS = HEAD_DIM + BF16_ROWS


def _fill_value_rows(vaug_sc, vt_ref, n_heads):
    ones = jnp.ones((BF16_ROWS, vt_ref.shape[1]), BF16)
    for h in range(n_heads):
        vaug_sc[h, 0:HEAD_DIM, :] = vt_ref[h * HEAD_DIM:(h + 1) * HEAD_DIM, :]
        vaug_sc[h, HEAD_DIM:, :] = ones


def _dual_pipeline(prefix, loop_tile, n_loop, extra, tk, n_heads, vaug_sc, slots, m_sc, acc_sc):
    def stage_q(tile, slot):
        s_sc, t_sc, _, _ = slots[slot]
        for h, s_t in enumerate(tile[0]()):
            s_sc[h] = s_t
            t_sc[h] = jnp.max(s_t, axis=0, keepdims=True)

    def stage_x(slot, state):
        s_sc, t_sc, p_sc, a_sc = slots[slot]
        for h in range(n_heads):
            m_prev = m_sc[state, h]
            m_new = jnp.maximum(m_prev, t_sc[h])
            a_sc[h] = jnp.exp2(m_prev - m_new)
            p_sc[h] = jnp.exp2(s_sc[h] - m_new).astype(BF16)
            m_sc[state, h] = m_new

    def stage_v(k0, slot, state):
        _, _, p_sc, a_sc = slots[slot]
        for h in range(n_heads):
            pv = jnp.dot(vaug_sc[h, :, pl.ds(k0, tk)], p_sc[h], preferred_element_type=F32)
            acc_sc[state, h] = a_sc[h] * acc_sc[state, h] + pv

    assert len(prefix) % 2 == 0
    m_sc[...] = jnp.full_like(m_sc, MASK_VALUE)
    acc_sc[...] = jnp.zeros_like(acc_sc)
    stage_q(prefix[0], 0)
    stage_q(prefix[1], 1)
    stage_x(0, 0)
    for u in range(1, len(prefix) - 1):
        stage_q(prefix[u + 1], (u + 1) % 2)
        stage_v(prefix[u - 1][1], (u - 1) % 2, (u - 1) % 2)
        stage_x(u % 2, u % 2)

    def pending_k0(state, k):
        return jnp.where(k == 0, prefix[len(prefix) - 2 + state][1],
                         loop_tile(state, jnp.maximum(k - 1, 0))[1])

    def pair_body(k, carry):
        stage_q(loop_tile(0, k), 0)
        stage_v(pending_k0(0, k), 0, 0)
        stage_x(1, 1)
        stage_q(loop_tile(1, k), 1)
        stage_v(pending_k0(1, k), 1, 1)
        stage_x(0, 0)
        return carry

    lax.fori_loop(0, n_loop, pair_body, 0)
    stage_q(extra, 0)
    stage_v(pending_k0(0, n_loop), 0, 0)
    stage_x(1, 1)
    stage_v(pending_k0(1, n_loop), 1, 1)
    stage_x(0, 1)
    stage_v(extra[1], 0, 1)


def _pipeline_scratch(n_heads, tile):
    one = [pltpu.VMEM((n_heads, tile, tile), F32), pltpu.VMEM((n_heads, 1, tile), F32),
           pltpu.VMEM((n_heads, tile, tile), BF16), pltpu.VMEM((n_heads, 1, tile), F32)]
    return one + one


def _softmax_finish(acc_sc, n_heads):
    blocks = []
    for pair in range(n_heads // 2):
        halves = []
        for h in (2 * pair, 2 * pair + 1):
            acc = acc_sc[h]
            halves.append(acc[0:HEAD_DIM, :] / acc[HEAD_DIM:HEAD_DIM + 1, :])
        out_t = jnp.concatenate(halves, axis=0)
        blocks.append(jnp.transpose(out_t).astype(BF16))
    return blocks


def _head_rows(qt_ref, h, q0, tq):
    pair, e = divmod(h, 2)
    q = qt_ref[pair * HEAD_PAIR_W:(pair + 1) * HEAD_PAIR_W, pl.ds(q0, tq)]
    row = lax.broadcasted_iota(jnp.int32, q.shape, 0)
    mine = (row >= e * HEAD_DIM) & (row < (e + 1) * HEAD_DIM)
    return jnp.where(mine, q, jnp.zeros_like(q))


def _key_le_query(tile):
    key = lax.broadcasted_iota(jnp.int32, (tile, tile), 0)
    query = lax.broadcasted_iota(jnp.int32, (tile, tile), 1)
    return key <= query


def _fox_kernel(qt_ref, k_ref, f_ref, vt_ref, o_ref,
                kaug_sc, vaug_sc, m_sc, acc_sc, *slot_refs, tile):
    g = pl.program_id(1)

    @pl.when(g == 0)
    def _():
        for pair in range(N_PAIR):
            kaug_sc[pair, :, 0:HEAD_PAIR_W] = k_ref[:, pair * HEAD_PAIR_W:(pair + 1) * HEAD_PAIR_W]
            kaug_sc[pair, :, HEAD_PAIR_W:] = f_ref[:, pair * LANES:(pair + 1) * LANES]
        _fill_value_rows(vaug_sc, vt_ref, H_FOX)

    row = lax.broadcasted_iota(jnp.int32, (LANES, tile), 0)
    q_aug = []
    for state in range(2):
        q_aug.append([])
        for h in range(H_FOX):
            e = h % 2
            minus_one = (row >= F_PIECES * e) & (row < F_PIECES * (e + 1))
            sel = jnp.where(minus_one, -1.0, 0.0).astype(BF16)
            q_aug[state].append(jnp.concatenate([_head_rows(qt_ref, h, state * tile, tile), sel],
                                                axis=0))

    def key_tile(state, j, masked):
        k0 = pl.multiple_of(j * tile, tile)

        def scores():
            tiles = [jnp.dot(kaug_sc[h // 2, pl.ds(k0, tile), :], q_aug[state][h],
                             preferred_element_type=F32) for h in range(H_FOX)]
            if masked:
                allowed = _key_le_query(tile)
                tiles = [jnp.where(allowed, s_t, MASK_VALUE) for s_t in tiles]
            return tiles
        return scores, k0

    _dual_pipeline(prefix=[key_tile(0, 2 * g, True), key_tile(1, 2 * g + 1, True)],
                   loop_tile=lambda state, k: key_tile(state, k, False), n_loop=2 * g,
                   extra=key_tile(1, 2 * g, False), tk=tile, n_heads=H_FOX, vaug_sc=vaug_sc,
                   slots=(slot_refs[:4], slot_refs[4:]), m_sc=m_sc, acc_sc=acc_sc)
    for state in range(2):
        for pair, block in enumerate(_softmax_finish(acc_sc.at[state], H_FOX)):
            o_ref[state * tile:(state + 1) * tile, pair * HEAD_PAIR_W:(pair + 1) * HEAD_PAIR_W] = block


def _fox_attention(qt, k, vt, faug):
    bsz, seq, _ = k.shape
    tile = min(ATT_TILE, seq)
    step = 2 * tile
    assert seq % step == 0 and tile % LANES == 0
    width = H_FOX * HEAD_DIM
    return pl.pallas_call(
        functools.partial(_fox_kernel, tile=tile),
        grid=(bsz, seq // step),
        in_specs=[pl.BlockSpec((None, width, step), lambda b, g: (b, 0, g)),
                  pl.BlockSpec((None, seq, width), lambda b, g: (b, 0, 0)),
                  pl.BlockSpec((None, seq, N_PAIR * LANES), lambda b, g: (b, 0, 0)),
                  pl.BlockSpec((None, width, seq), lambda b, g: (b, 0, 0))],
        out_specs=pl.BlockSpec((None, step, width), lambda b, g: (b, g, 0)),
        out_shape=jax.ShapeDtypeStruct((bsz, seq, width), BF16),
        scratch_shapes=[pltpu.VMEM((N_PAIR, seq, 2 * HEAD_PAIR_W), BF16),
                        pltpu.VMEM((H_FOX, ACC_ROWS, seq), BF16),
                        pltpu.VMEM((2, H_FOX, 1, tile), F32),
                        pltpu.VMEM((2, H_FOX, ACC_ROWS, tile), F32)]
        + _pipeline_scratch(H_FOX, tile),
        compiler_params=_params(("arbitrary", "arbitrary")),
        name="fox_attention",
    )(qt, k, faug, vt)


def _t5_bias_tile(dist, rel_ref, head):
    exact = T5_BUCKETS // 2
    d_f = jnp.maximum(dist, 1).astype(F32)
    log_b = exact + (jnp.log(d_f / exact) / math.log(T5_MAX_DIST / exact)
                     * (T5_BUCKETS - exact)).astype(jnp.int32)
    log_b = jnp.minimum(log_b, T5_BUCKETS - 1)
    bucket = jnp.where(dist < exact, dist, log_b)
    out = jnp.zeros(dist.shape, F32)
    for b in range(T5_BUCKETS):
        out = jnp.where(bucket == b, rel_ref[b, head], out)
    return out


def _moba_kernel(rel_ref, qt_ref, k_ref, vt_ref, h_ref, yc_ref, wo_ref, o_ref,
                 kmean_sc, vaug_sc, bias_sc, pen_sc, m_sc, acc_sc, *slot_refs, seq, blk_rows):
    blk = MOBA_BLOCK
    b = pl.program_id(0)
    g = pl.program_id(1)

    @pl.when((b == 0) & (g == 0))
    def _():
        key = lax.broadcasted_iota(jnp.int32, (blk, blk), 0)
        query = lax.broadcasted_iota(jnp.int32, (blk, blk), 1)
        for h in range(H_MOBA):
            for delta in range(2):
                dist = jnp.maximum(delta * blk + query - key, 0)
                bias_sc[h, delta] = _t5_bias_tile(dist, rel_ref, h) * LOG2E

    @pl.when(g == 0)
    def _():
        mrow = lax.broadcasted_iota(jnp.int32, (blk_rows, seq), 0)
        mcol = lax.broadcasted_iota(jnp.int32, (blk_rows, seq), 1) // blk
        avg = jnp.where(mrow == mcol, 1.0 / blk, 0.0).astype(BF16)
        kmean_sc[...] = jnp.dot(avg, k_ref[...], preferred_element_type=F32)
        _fill_value_rows(vaug_sc, vt_ref, H_MOBA)

    own = (2 * g, 2 * g + 1)
    qh = [[_head_rows(qt_ref, h, state * blk, blk) for h in range(H_MOBA)] for state in range(2)]

    blk_id = lax.broadcasted_iota(jnp.int32, (blk_rows, blk), 0)
    q_far = [[], []]
    for state in range(2):
        for h in range(H_MOBA):
            pair = h // 2
            kmean = kmean_sc[:, pair * HEAD_PAIR_W:(pair + 1) * HEAD_PAIR_W]
            km_hi = kmean.astype(BF16)
            km_lo = (kmean - km_hi.astype(F32)).astype(BF16)
            gate = (jnp.dot(km_hi, qh[state][h], preferred_element_type=F32)
                    + jnp.dot(km_lo, qh[state][h], preferred_element_type=F32))
            gate = jnp.where(blk_id < own[state], gate, -jnp.inf)
            sel = jnp.zeros(gate.shape, jnp.bool_)
            for _ in range(MOBA_TOPK):
                best = jnp.max(gate, axis=0, keepdims=True)
                first_id = jnp.min(jnp.where(gate == best, blk_id, blk_rows), axis=0, keepdims=True)
                pick = (blk_id == first_id) & (best > -jnp.inf)
                sel = sel | pick
                gate = jnp.where(pick, -jnp.inf, gate)
            pen = jnp.where(sel, 0.0, MASK_VALUE)
            pen_sc[state, h] = pen
            far = pen + rel_ref[T5_BUCKETS - 1, h] * LOG2E
            far_hi = far.astype(BF16)
            far_lo = (far - far_hi.astype(F32)).astype(BF16)
            pad = jnp.zeros((HEAD_PAIR_W - 2 * blk_rows, blk), BF16)
            q_far[state].append(jnp.concatenate([qh[state][h], far_hi, far_lo, pad], axis=0))

    def key_rows(j, h):
        pair = h // 2
        return k_ref[pl.ds(pl.multiple_of(j * blk, blk), blk),
                     pair * HEAD_PAIR_W:(pair + 1) * HEAD_PAIR_W]

    def own_block(state):
        j = own[state]

        def scores():
            allowed = _key_le_query(blk)
            return [jnp.where(allowed, jnp.dot(key_rows(j, h), qh[state][h],
                                               preferred_element_type=F32) + bias_sc[h, 0],
                              MASK_VALUE) for h in range(H_MOBA)]
        return scores, pl.multiple_of(j * blk, blk)

    def previous_block(state):
        j = jnp.maximum(own[state] - 1, 0)

        def scores():
            return [jnp.dot(key_rows(j, h), qh[state][h], preferred_element_type=F32)
                    + bias_sc[h, 1] + pen_sc[state, h, pl.ds(j, 1), :] for h in range(H_MOBA)]
        return scores, pl.multiple_of(j * blk, blk)

    def older_block(state, j, present=None):
        def scores():
            lane = lax.broadcasted_iota(jnp.int32, (blk, HEAD_PAIR_W), 1)
            one_hot = jnp.where((lane == j) | (lane == blk_rows + j), 1.0, 0.0).astype(BF16)
            tiles = [jnp.dot(jnp.concatenate([key_rows(j, h), one_hot], axis=1), q_far[state][h],
                             preferred_element_type=F32) for h in range(H_MOBA)]
            if present is not None:
                absent = jnp.where(present, 0.0, MASK_VALUE)
                tiles = [s_t + absent for s_t in tiles]
            return tiles
        return scores, pl.multiple_of(j * blk, blk)

    _dual_pipeline(prefix=[own_block(0), own_block(1), previous_block(0), previous_block(1)],
                   loop_tile=lambda state, k: older_block(state, k),
                   n_loop=jnp.maximum(2 * g - 1, 0),
                   extra=older_block(1, jnp.maximum(2 * g - 1, 0), present=g >= 1),
                   tk=blk, n_heads=H_MOBA, vaug_sc=vaug_sc,
                   slots=(slot_refs[:4], slot_refs[4:]), m_sc=m_sc, acc_sc=acc_sc)

    wc = yc_ref.shape[-1]
    for state in range(2):
        y_d = jnp.concatenate(_softmax_finish(acc_sc.at[state], H_MOBA), axis=-1)
        rows = slice(state * blk, (state + 1) * blk)
        o_ref[rows, :] = (h_ref[rows, :]
                          + jnp.dot(yc_ref[rows, :], wo_ref[0:wc, :], preferred_element_type=F32)
                          + jnp.dot(y_d, wo_ref[wc:, :], preferred_element_type=F32))


def _moba_attention_out(qt, k, vt, rel_bias, h, y_c, w_out, o):
    bsz, seq, d = h.shape
    blk = MOBA_BLOCK
    step = 2 * blk
    assert seq % step == 0
    assert blk + 1 >= T5_MAX_DIST
    blk_rows = -(-(seq // blk) // BF16_ROWS) * BF16_ROWS
    width = H_MOBA * HEAD_DIM
    off = (H_FOX * HEAD_DIM) // width
    return pl.pallas_call(
        functools.partial(_moba_kernel, seq=seq, blk_rows=blk_rows),
        grid_spec=pltpu.PrefetchScalarGridSpec(
            num_scalar_prefetch=1,
            grid=(bsz, seq // step),
            in_specs=[pl.BlockSpec((None, width, step), lambda b, g, rel: (b, off, g)),
                      pl.BlockSpec((None, seq, width), lambda b, g, rel: (b, 0, off)),
                      pl.BlockSpec((None, width, seq), lambda b, g, rel: (b, off, 0)),
                      pl.BlockSpec((None, step, d), lambda b, g, rel: (b, g, 0)),
                      pl.BlockSpec((None, step, y_c.shape[-1]), lambda b, g, rel: (b, g, 0)),
                      pl.BlockSpec((None,) + w_out.shape[1:], lambda b, g, rel: (o, 0, 0),
                                   pipeline_mode=pl.Buffered(1))],
            out_specs=pl.BlockSpec((None, step, d), lambda b, g, rel: (b, g, 0)),
            scratch_shapes=[pltpu.VMEM((blk_rows, width), F32),
                            pltpu.VMEM((H_MOBA, ACC_ROWS, seq), BF16),
                            pltpu.VMEM((H_MOBA, 2, blk, blk), F32),
                            pltpu.VMEM((2, H_MOBA, blk_rows, blk), F32),
                            pltpu.VMEM((2, H_MOBA, 1, blk), F32),
                            pltpu.VMEM((2, H_MOBA, ACC_ROWS, blk), F32)]
            + _pipeline_scratch(H_MOBA, blk)),
        out_shape=jax.ShapeDtypeStruct(h.shape, F32),
        compiler_params=_params(("arbitrary", "arbitrary")),
        name="moba_attention_out",
    )(rel_bias, qt, k, vt, h, y_c, w_out)


def kernel(x, mix_norm_g, ffn_norm_g, final_norm_g, ev_w_in, ev_conv_w, ev_pool_w, ev_pool_scale,
           ev_w_out, od_w_in, od_b_f, od_w_out, rel_bias, ffn_w_in, ffn_conv_w, ffn_conv_b, ffn_w_out):
    depth, d = mix_norm_g.shape
    mix_g = mix_norm_g.reshape(depth, 1, d)
    ffn_g = ffn_norm_g.reshape(depth, 1, d)
    ev_w_in_b, ev_pool_w_b, ev_w_out_b = (w.astype(BF16) for w in (ev_w_in, ev_pool_w, ev_w_out))
    ev_scale = ev_pool_scale.reshape(ev_pool_scale.shape[0], 1, B_W)
    od_w_out_b = od_w_out.astype(BF16)
    ffn_b = ffn_conv_b.reshape(depth, 1, D_FF)
    h = x
    for layer in range(depth):
        if layer % 2 == 0:
            outs = _even_layer(h, layer, layer // 2, mix_g, ev_w_in_b, ev_conv_w, ev_pool_w_b, ev_scale,
                               ev_w_out_b, ffn_w=(ffn_w_in, ffn_w_out) if layer == 0 else None)
            h = outs[0]
            if layer == 0:
                ffn_w_in_b, ffn_w_out_b = outs[1], outs[2]
        else:
            o = layer // 2
            qt, k, vt, zf = _qkv_proj(h, mix_norm_g[layer], od_w_in[o])
            faug = _fgate_cumsum(zf, od_b_f[o])
            y_c = _fox_attention(qt, k, vt, faug)
            h = _moba_attention_out(qt, k, vt, rel_bias, h, y_c, od_w_out_b, o)
        last = layer == depth - 1
        outs = _ffn_layer(h, layer, ffn_g, ffn_w_in_b, ffn_w_out_b, 0, ffn_conv_w, ffn_b, final_norm_g,
                          final_norm=last, next_w_in=None if last else ffn_w_in,
                          next_w_out=None if last else ffn_w_out)
        h = outs[0]
        if not last:
            ffn_w_in_b, ffn_w_out_b = outs[1], outs[2]
    return h
```
